```python
import math
import jax, jax.numpy as jnp
from jax import lax
import numpy as np

D_MODEL = 1024
BATCH = 4
SEQ = 4096
DEPTH = 2

N_MIXERS = 2
EXPAND = 2
D_INNER = EXPAND * D_MODEL
EPS = 1e-6

DA_HEADS = 16
DA_HEAD_DIM = D_INNER // (2 * DA_HEADS)
DA_V_DIM = 2 * DA_HEAD_DIM
ROPE_THETA = 10000.0
Q_BLOCK = 128
LAMBDA_STD = 0.1

GLA_HEADS = 4
GLA_DK = D_INNER // 2
GLA_HEAD_K = GLA_DK // GLA_HEADS
GLA_HEAD_V = D_INNER // GLA_HEADS
GLA_GATE_RANK = 16
GLA_GATE_TAU = 16.0
GLA_CHUNK = 64
GLA_IN = 2 * GLA_DK + 2 * D_INNER + GLA_GATE_RANK

N_DIFF = (DEPTH + 1) // 2
N_GLA = DEPTH // 2
ADA_STD = 0.2

kernel_name = "hybrid_diffattn_gla_adaln"


def rmsnorm(x, w):
    x32 = x.astype(jnp.float32)
    y = x32 * lax.rsqrt(jnp.mean(x32 * x32, axis=-1, keepdims=True) + EPS)
    return (y * w.astype(jnp.float32)).astype(x.dtype)


def rope_tables(positions):
    inv_freq = ROPE_THETA ** (-jnp.arange(0, DA_HEAD_DIM, 2, dtype=jnp.float32) / DA_HEAD_DIM)
    ang = positions.astype(jnp.float32)[..., None] * inv_freq
    ang = jnp.concatenate([ang, ang], axis=-1)[:, None]
    return jnp.cos(ang), jnp.sin(ang)


def apply_rope(t, cos, sin):
    t1, t2 = jnp.split(t, 2, axis=-1)
    rot = jnp.concatenate([-t2, t1], axis=-1)
    return (t * cos + rot * sin).astype(t.dtype)


def diff_attention_mixer(h, cos, sin, w_in, lq1, lk1, lq2, lk2, subln_w, w_out, lambda_init):
    B, S, _ = h.shape
    H, d = DA_HEADS, DA_HEAD_DIM
    proj = h @ w_in
    q, k, v, z = jnp.split(proj, 4, axis=-1)
    q = q.reshape(B, S, 2 * H, d).transpose(0, 2, 1, 3)
    k = k.reshape(B, S, 2 * H, d).transpose(0, 2, 1, 3)
    v = v.reshape(B, S, H, DA_V_DIM).transpose(0, 2, 1, 3)
    q = apply_rope(q, cos, sin) * (d ** -0.5)
    k = apply_rope(k, cos, sin)
    q = q.reshape(B, H, 2, S, d)
    k = k.reshape(B, H, 2, S, d)
    lam = (jnp.exp(jnp.sum(lq1.astype(jnp.float32) * lk1.astype(jnp.float32)))
           - jnp.exp(jnp.sum(lq2.astype(jnp.float32) * lk2.astype(jnp.float32)))
           + lambda_init)
    nb = S // Q_BLOCK
    qb = q.reshape(B, H, 2, nb, Q_BLOCK, d).transpose(3, 0, 1, 2, 4, 5)
    k_pos = jnp.arange(S)
    neg = jnp.finfo(jnp.float32).min

    def block(args):
        q_blk, i = args
        s = jnp.einsum('bhcqd,bhckd->bhcqk', q_blk, k).astype(jnp.float32)
        q_pos = i * Q_BLOCK + jnp.arange(Q_BLOCK)
        mask = k_pos[None, :] <= q_pos[:, None]
        p = jax.nn.softmax(jnp.where(mask, s, neg), axis=-1)
        a = p[:, :, 0] - lam * p[:, :, 1]
        return jnp.einsum('bhqk,bhkv->bhqv', a.astype(v.dtype), v)

    o = lax.map(block, (qb, jnp.arange(nb)))
    o = o.transpose(1, 0, 3, 2, 4).reshape(B, S, H, DA_V_DIM)
    o = rmsnorm(o, subln_w) * (1.0 - lambda_init)
    o = o.reshape(B, S, D_INNER) * jax.nn.silu(z)
    return (o @ w_out).astype(h.dtype)


def gla_mixer(h, w_in, w_gate_up, b_gate, norm_w, w_out):
    B, S, _ = h.shape
    H, C = GLA_HEADS, GLA_CHUNK
    N = S // C
    proj = h @ w_in
    q, k, v, z, g_low = jnp.split(
        proj, [GLA_DK, 2 * GLA_DK, 2 * GLA_DK + D_INNER, 2 * GLA_DK + 2 * D_INNER], axis=-1)
    log_a = jax.nn.log_sigmoid((g_low @ w_gate_up + b_gate).astype(jnp.float32)) / GLA_GATE_TAU

    def heads(t, dh):
        return t.reshape(B, N, C, H, dh).transpose(0, 3, 1, 2, 4)

    q = heads(q, GLA_HEAD_K) * (GLA_HEAD_K ** -0.5)
    k = heads(k, GLA_HEAD_K)
    v = heads(v, GLA_HEAD_V)
    b = jnp.cumsum(heads(log_a, GLA_HEAD_K), axis=3)
    b_last = b[:, :, :, -1:, :]
    q_dec = q * jnp.exp(b)
    k_inv = k * jnp.exp(-b)
    k_end = k * jnp.exp(b_last - b)
    tril = jnp.tril(jnp.ones((C, C), dtype=bool))
    attn = jnp.where(tril, jnp.einsum('bhnid,bhnjd->bhnij', q_dec, k_inv), 0.0)
    o_intra = jnp.einsum('bhnij,bhnjv->bhniv', attn, v)

    def step(state, inp):
        q_c, k_c, v_c, dec_c = inp
        o_c = jnp.einsum('bhcd,bhdv->bhcv', q_c, state)
        state = state * dec_c[..., None] + jnp.einsum('bhcd,bhcv->bhdv', k_c, v_c)
        return state, o_c

    xs = (jnp.moveaxis(q_dec, 2, 0), jnp.moveaxis(k_end, 2, 0),
          jnp.moveaxis(v, 2, 0), jnp.moveaxis(jnp.exp(b_last[:, :, :, 0, :]), 2, 0))
    state0 = jnp.zeros((B, H, GLA_HEAD_K, GLA_HEAD_V), jnp.float32)
    _, o_inter = lax.scan(step, state0, xs)
    o = (o_intra + jnp.moveaxis(o_inter, 0, 2)).astype(h.dtype)
    o = o.transpose(0, 2, 3, 1, 4)
    o = rmsnorm(o, norm_w).reshape(B, S, D_INNER) * jax.nn.silu(z)
    return (o @ w_out).astype(h.dtype)


def setup_inputs(seed: int = 0) -> dict:
    key = jax.random.key(seed)
    ks = jax.random.split(key, 20)
    nrm = jax.random.normal
    f32 = jnp.float32
    x = nrm(ks[0], (BATCH, SEQ, D_MODEL), f32)
    c = nrm(ks[1], (BATCH, D_MODEL), f32)
    offset = jax.random.randint(ks[2], (BATCH, 1), 0, 1024, dtype=jnp.int32)
    positions = offset + jnp.arange(SEQ, dtype=jnp.int32)[None, :]
    ada_w = nrm(ks[3], (DEPTH, D_MODEL, 3 * D_MODEL), f32) * (ADA_STD * D_MODEL ** -0.5)
    ada_b = nrm(ks[4], (DEPTH, 3 * D_MODEL), f32) * 0.02
    norm_g = 1.0 + 0.02 * nrm(ks[5], (DEPTH, D_MODEL), f32)
    da_w_in = nrm(ks[6], (N_DIFF, D_MODEL, 4 * D_INNER), f32) * D_MODEL ** -0.5
    da_lam_q1 = nrm(ks[7], (N_DIFF, DA_HEAD_DIM), f32) * LAMBDA_STD
    da_lam_k1 = nrm(ks[8], (N_DIFF, DA_HEAD_DIM), f32) * LAMBDA_STD
    da_lam_q2 = nrm(ks[9], (N_DIFF, DA_HEAD_DIM), f32) * LAMBDA_STD
    da_lam_k2 = nrm(ks[10], (N_DIFF, DA_HEAD_DIM), f32) * LAMBDA_STD
    da_subln_w = 1.0 + 0.02 * nrm(ks[11], (N_DIFF, DA_V_DIM), f32)
    da_w_out = nrm(ks[12], (N_DIFF, D_INNER, D_MODEL), f32) * D_INNER ** -0.5
    gla_w_in = nrm(ks[13], (N_GLA, D_MODEL, GLA_IN), f32) * D_MODEL ** -0.5
    gla_w_gate_up = nrm(ks[14], (N_GLA, GLA_GATE_RANK, GLA_DK), f32) * GLA_GATE_RANK ** -0.5
    gla_b_gate = nrm(ks[15], (N_GLA, GLA_DK), f32) * 0.1
    gla_norm_w = 1.0 + 0.02 * nrm(ks[16], (N_GLA, GLA_HEAD_V), f32)
    gla_w_out = nrm(ks[17], (N_GLA, D_INNER, D_MODEL), f32) * D_INNER ** -0.5
    final_g = 1.0 + 0.02 * nrm(ks[18], (D_MODEL,), f32)
    return {"x": x, "c": c, "positions": positions, "ada_w": ada_w, "ada_b": ada_b,
            "norm_g": norm_g, "da_w_in": da_w_in, "da_lam_q1": da_lam_q1,
            "da_lam_k1": da_lam_k1, "da_lam_q2": da_lam_q2, "da_lam_k2": da_lam_k2,
            "da_subln_w": da_subln_w, "da_w_out": da_w_out, "gla_w_in": gla_w_in,
            "gla_w_gate_up": gla_w_gate_up, "gla_b_gate": gla_b_gate,
            "gla_norm_w": gla_norm_w, "gla_w_out": gla_w_out, "final_g": final_g}


def reference(x, c, positions, ada_w, ada_b, norm_g, da_w_in, da_lam_q1, da_lam_k1,
              da_lam_q2, da_lam_k2, da_subln_w, da_w_out, gla_w_in, gla_w_gate_up,
              gla_b_gate, gla_norm_w, gla_w_out, final_g):
    cos, sin = rope_tables(positions)
    for i in range(DEPTH):
        mod = c @ ada_w[i] + ada_b[i]
        shift, scale, gate = jnp.split(mod, 3, axis=-1)
        h = rmsnorm(x, norm_g[i]) * (1.0 + scale[:, None, :]) + shift[:, None, :]
        j = i // N_MIXERS
        if i % N_MIXERS == 0:
            lambda_init = 0.8 - 0.6 * math.exp(-0.3 * i)
            out = diff_attention_mixer(h, cos, sin, da_w_in[j], da_lam_q1[j], da_lam_k1[j],
                                       da_lam_q2[j], da_lam_k2[j], da_subln_w[j],
                                       da_w_out[j], lambda_init)
        else:
            out = gla_mixer(h, gla_w_in[j], gla_w_gate_up[j], gla_b_gate[j],
                            gla_norm_w[j], gla_w_out[j])
        x = x + gate[:, None, :] * out
    return rmsnorm(x, final_g)
```

```python
import functools
import math

import jax
import jax.numpy as jnp
from jax import lax
from jax.experimental import pallas as pl
from jax.experimental.pallas import tpu as pltpu

F32 = jnp.float32
BF16 = jnp.bfloat16

D_MODEL = 1024
BATCH = 4
SEQ = 4096
DEPTH = 2
D_INNER = 2 * D_MODEL
EPS = 1e-6
M_ROWS = BATCH * SEQ

DA_HEADS = 16
DA_HEAD_DIM = 64
DA_V_DIM = 128
ROPE_THETA = 10000.0
ROPE_HALF = DA_HEAD_DIM // 2

GLA_HEADS = 4
GLA_DK = 1024
GLA_HEAD_K = 256
GLA_HEAD_V = 512
GLA_GATE_RANK = 16
GLA_GATE_TAU = 16.0
GLA_CHUNK = 64
GLA_MAIN = 2 * GLA_DK + 2 * D_INNER

LANES = 128
ADA_ROWS = 16
VMEM_LIMIT = 56 * 1024 * 1024

TM_IN, TN_IN = 1024, 1024
TM_OUT = 512
TQ = 512
GLA_ROWS = 256
TM_ROPE = 1024
TN_ADA = 1536

NEG = -1e30


def _cparams(sem):
    return pltpu.CompilerParams(dimension_semantics=sem, vmem_limit_bytes=VMEM_LIMIT)


def _ada_kernel(c_ref, w_ref, b_ref, o_ref):
    o_ref[...] = jnp.dot(c_ref[...], w_ref[...].astype(BF16),
                         preferred_element_type=F32) + b_ref[...]


def _ada(c_pad, ada_w, ada_b3):
    n = 3 * D_MODEL
    return pl.pallas_call(
        _ada_kernel,
        grid=(DEPTH, n // TN_ADA),
        in_specs=[
            pl.BlockSpec((ADA_ROWS, D_MODEL), lambda l, j: (0, 0)),
            pl.BlockSpec((None, D_MODEL, TN_ADA), lambda l, j: (l, 0, j)),
            pl.BlockSpec((None, 1, TN_ADA), lambda l, j: (l, 0, j)),
        ],
        out_specs=pl.BlockSpec((None, ADA_ROWS, TN_ADA), lambda l, j: (l, 0, j)),
        out_shape=jax.ShapeDtypeStruct((DEPTH, ADA_ROWS, n), F32),
        compiler_params=_cparams(("parallel", "parallel")),
    )(c_pad, ada_w, ada_b3)


def _rope_kernel(pos_ref, invf_ref, cos_ref, sin_ref):
    ang = invf_ref[...] * pos_ref[...].astype(F32)
    c = jnp.cos(ang)
    s = jnp.sin(ang)
    cos_ref[...] = jnp.concatenate([c, c, c, c], axis=0).T
    sin_ref[...] = jnp.concatenate([-s, s, -s, s], axis=0).T


def _rope_tables(pos3, invf):
    nt = SEQ // TM_ROPE
    return pl.pallas_call(
        _rope_kernel,
        grid=(BATCH, nt),
        in_specs=[
            pl.BlockSpec((None, 1, TM_ROPE), lambda b, t: (b, 0, t)),
            pl.BlockSpec((ROPE_HALF, 1), lambda b, t: (0, 0)),
        ],
        out_specs=[
            pl.BlockSpec((TM_ROPE, LANES), lambda b, t: (b * nt + t, 0)),
            pl.BlockSpec((TM_ROPE, LANES), lambda b, t: (b * nt + t, 0)),
        ],
        out_shape=[jax.ShapeDtypeStruct((M_ROWS, LANES), F32)] * 2,
        compiler_params=_cparams(("parallel", "parallel")),
    )(pos3, invf)


def _norm_modulate(x, g, scale, shift):
    ms = jnp.mean(x * x, axis=-1, keepdims=True)
    return x * lax.rsqrt(ms + EPS) * (g * (1.0 + scale)) + shift


def _inproj_da_kernel(x_ref, g_ref, scale_ref, shift_ref, w_ref, cos_ref, sin_ref,
                      o_ref, h_scr, *, n_q_tiles, n_rope_tiles, q_scale):
    j = pl.program_id(1)

    @pl.when(j == 0)
    def _():
        h_scr[...] = _norm_modulate(x_ref[...], g_ref[...], scale_ref[...],
                                    shift_ref[...]).astype(BF16)

    acc = jnp.dot(h_scr[...], w_ref[...], preferred_element_type=F32)

    @pl.when(j < n_rope_tiles)
    def _():
        sc = jnp.where(j < n_q_tiles, q_scale, 1.0).astype(F32)
        cos = cos_ref[...] * sc
        sin = sin_ref[...] * sc
        lane = lax.broadcasted_iota(jnp.int32, cos.shape, 1)
        first_half = (lane % DA_HEAD_DIM) < ROPE_HALF
        for gidx in range(acc.shape[1] // LANES):
            t = acc[:, gidx * LANES:(gidx + 1) * LANES]
            rot = jnp.where(first_half, pltpu.roll(t, LANES - ROPE_HALF, 1),
                            pltpu.roll(t, ROPE_HALF, 1))
            o_ref[:, gidx * LANES:(gidx + 1) * LANES] = (t * cos + rot * sin).astype(BF16)

    @pl.when(j >= n_rope_tiles)
    def _():
        o_ref[...] = acc.astype(BF16)


def _inproj_da(x2, norm_g, mod, w_bf, cos, sin, layer):
    n = w_bf.shape[1]
    seg_tiles = D_INNER // TN_IN
    tiles_per_batch = SEQ // TM_IN

    def mod_idx(which):
        return lambda i, j: ((layer * ADA_ROWS + i // tiles_per_batch) * 3 + which, 0, 0)

    kern = functools.partial(_inproj_da_kernel, n_q_tiles=seg_tiles,
                             n_rope_tiles=2 * seg_tiles, q_scale=DA_HEAD_DIM ** -0.5)
    return pl.pallas_call(
        kern,
        grid=(M_ROWS // TM_IN, n // TN_IN),
        in_specs=[
            pl.BlockSpec((TM_IN, D_MODEL), lambda i, j: (i, 0)),
            pl.BlockSpec((None, 1, D_MODEL), lambda i, j: (layer, 0, 0)),
            pl.BlockSpec((None, 1, D_MODEL), mod_idx(1)),
            pl.BlockSpec((None, 1, D_MODEL), mod_idx(0)),
            pl.BlockSpec((D_MODEL, TN_IN), lambda i, j: (0, j)),
            pl.BlockSpec((TM_IN, LANES), lambda i, j: (i, 0)),
            pl.BlockSpec((TM_IN, LANES), lambda i, j: (i, 0)),
        ],
        out_specs=pl.BlockSpec((None, TM_IN, TN_IN),
                               lambda i, j: (j // seg_tiles, i, j % seg_tiles)),
        out_shape=jax.ShapeDtypeStruct((4, M_ROWS, D_INNER), BF16),
        scratch_shapes=[pltpu.VMEM((TM_IN, D_MODEL), BF16)],
        compiler_params=_cparams(("parallel", "arbitrary")),
    )(x2, norm_g, mod, mod, w_bf, cos, sin)


def _inproj_gla_kernel(x_ref, g_ref, scale_ref, shift_ref, w_ref, wg_ref,
                       o_ref, glow_ref, h_scr):
    j = pl.program_id(1)

    @pl.when(j == 0)
    def _():
        h = _norm_modulate(x_ref[...], g_ref[...], scale_ref[...], shift_ref[...]).astype(BF16)
        h_scr[...] = h
        glow_ref[...] = jnp.dot(h, wg_ref[...], preferred_element_type=F32)

    o_ref[...] = jnp.dot(h_scr[...], w_ref[...], preferred_element_type=F32).astype(BF16)


def _inproj_gla(x2, norm_g, mod, w_bf, wg_bf, layer):
    n = w_bf.shape[1]
    tiles_per_batch = SEQ // TM_IN

    def mod_idx(which):
        return lambda i, j: ((layer * ADA_ROWS + i // tiles_per_batch) * 3 + which, 0, 0)

    return pl.pallas_call(
        _inproj_gla_kernel,
        grid=(M_ROWS // TM_IN, n // TN_IN),
        in_specs=[
            pl.BlockSpec((TM_IN, D_MODEL), lambda i, j: (i, 0)),
            pl.BlockSpec((None, 1, D_MODEL), lambda i, j: (layer, 0, 0)),
            pl.BlockSpec((None, 1, D_MODEL), mod_idx(1)),
            pl.BlockSpec((None, 1, D_MODEL), mod_idx(0)),
            pl.BlockSpec((D_MODEL, TN_IN), lambda i, j: (0, j)),
            pl.BlockSpec((D_MODEL, LANES), lambda i, j: (0, 0)),
        ],
        out_specs=[
            pl.BlockSpec((TM_IN, TN_IN), lambda i, j: (i, j)),
            pl.BlockSpec((TM_IN, LANES), lambda i, j: (i, 0)),
        ],
        out_shape=[jax.ShapeDtypeStruct((M_ROWS, n), BF16),
                   jax.ShapeDtypeStruct((M_ROWS, LANES), F32)],
        scratch_shapes=[pltpu.VMEM((TM_IN, D_MODEL), BF16)],
        compiler_params=_cparams(("parallel", "arbitrary")),
    )(x2, norm_g, mod, mod, w_bf, wg_bf)


def _outproj_kernel(o_ref, w_ref, x_ref, gate_ref, fg_ref, out_ref, *, final_norm):
    y = jnp.dot(o_ref[...], w_ref[...], preferred_element_type=F32)
    x = x_ref[...] + gate_ref[...] * y
    if final_norm:
        ms = jnp.mean(x * x, axis=-1, keepdims=True)
        x = x * lax.rsqrt(ms + EPS) * fg_ref[...]
    out_ref[...] = x


def _outproj(o, w_bf, x2, mod, final_g, layer, final_norm):
    tiles_per_batch = SEQ // TM_OUT
    kern = functools.partial(_outproj_kernel, final_norm=final_norm)
    return pl.pallas_call(
        kern,
        grid=(M_ROWS // TM_OUT,),
        in_specs=[
            pl.BlockSpec((TM_OUT, D_INNER), lambda i: (i, 0)),
            pl.BlockSpec((D_INNER, D_MODEL), lambda i: (0, 0)),
            pl.BlockSpec((TM_OUT, D_MODEL), lambda i: (i, 0)),
            pl.BlockSpec((None, 1, D_MODEL),
                         lambda i: ((layer * ADA_ROWS + i // tiles_per_batch) * 3 + 2, 0, 0)),
            pl.BlockSpec((1, D_MODEL), lambda i: (0, 0)),
        ],
        out_specs=pl.BlockSpec((TM_OUT, D_MODEL), lambda i: (i, 0)),
        out_shape=jax.ShapeDtypeStruct((M_ROWS, D_MODEL), F32),
        compiler_params=_cparams(("parallel",)),
    )(o, w_bf, x2, mod, final_g)


def _diff_attn_kernel(q_ref, k_ref, v_ref, z_ref, lamv_ref, subw_ref, o_ref,
                      vt_scr, m_scr, l_scr, acc_scr, *, lambda_init):
    qi = pl.program_id(2)

    @pl.when(qi == 0)
    def _():
        vt_scr[...] = v_ref[...].astype(F32).T.astype(BF16)

    q = q_ref[...]
    lane = lax.broadcasted_iota(jnp.int32, q.shape, 1)
    zero = jnp.zeros_like(q)
    qs = jnp.concatenate([jnp.where(lane < DA_HEAD_DIM, q, zero),
                          jnp.where(lane >= DA_HEAD_DIM, q, zero)], axis=0)

    m_scr[...] = jnp.full(m_scr.shape, NEG, F32)
    l_scr[...] = jnp.zeros(l_scr.shape, F32)
    acc_scr[...] = jnp.zeros(acc_scr.shape, F32)

    def step(kc, masked):
        start = pl.multiple_of(kc * TQ, TQ)
        k = k_ref[pl.ds(start, TQ), :]
        st = lax.dot_general(k, qs, (((1,), (1,)), ((), ())),
                             preferred_element_type=F32)
        if masked:
            kpos = lax.broadcasted_iota(jnp.int32, st.shape, 0)
            qpos = lax.broadcasted_iota(jnp.int32, st.shape, 1) % TQ
            st = jnp.where(kpos <= qpos, st, NEG)
        m_prev = m_scr[...]
        m_new = jnp.maximum(m_prev, jnp.max(st, axis=0, keepdims=True))
        alpha = jnp.exp(m_prev - m_new)
        p = jnp.exp(st - m_new)
        l_scr[...] = alpha * l_scr[...] + jnp.sum(p, axis=0, keepdims=True)
        vt = vt_scr[:, pl.ds(start, TQ)]
        acc_scr[...] = alpha * acc_scr[...] + jnp.dot(vt, p.astype(BF16),
                                                      preferred_element_type=F32)
        m_scr[...] = m_new

    def body(kc, carry):
        step(kc, False)
        return carry

    lax.fori_loop(0, qi, body, 0)
    step(qi, True)

    lamv = lamv_ref[...]
    lam = (jnp.exp(jnp.sum(lamv[0:1, :] * lamv[1:2, :], keepdims=True))
           - jnp.exp(jnp.sum(lamv[2:3, :] * lamv[3:4, :], keepdims=True)) + lambda_init)
    on = acc_scr[...] / l_scr[...]
    ot = on[:, :TQ] - lam * on[:, TQ:]
    ms = jnp.mean(ot * ot, axis=0, keepdims=True)
    ot = ot * lax.rsqrt(ms + EPS) * (subw_ref[...] * (1.0 - lambda_init))
    z = z_ref[...].astype(F32)
    o_ref[...] = (ot.T * (z * jax.nn.sigmoid(z))).astype(BF16)


def _diff_attn(qkvz, lamv, subw_col, lambda_init):
    nq = SEQ // TQ
    kern = functools.partial(_diff_attn_kernel, lambda_init=lambda_init)
    return pl.pallas_call(
        kern,
        grid=(BATCH, DA_HEADS, nq),
        in_specs=[
            pl.BlockSpec((None, TQ, LANES), lambda b, h, i: (0, b * nq + i, h)),
            pl.BlockSpec((None, SEQ, LANES), lambda b, h, i: (1, b, h)),
            pl.BlockSpec((None, SEQ, LANES), lambda b, h, i: (2, b, h)),
            pl.BlockSpec((None, TQ, LANES), lambda b, h, i: (3, b * nq + i, h)),
            pl.BlockSpec((8, LANES), lambda b, h, i: (0, 0)),
            pl.BlockSpec((DA_V_DIM, 1), lambda b, h, i: (0, 0)),
        ],
        out_specs=pl.BlockSpec((TQ, LANES), lambda b, h, i: (b * nq + i, h)),
        out_shape=jax.ShapeDtypeStruct((M_ROWS, D_INNER), BF16),
        scratch_shapes=[
            pltpu.VMEM((DA_V_DIM, SEQ), BF16),
            pltpu.VMEM((1, 2 * TQ), F32),
            pltpu.VMEM((1, 2 * TQ), F32),
            pltpu.VMEM((DA_V_DIM, 2 * TQ), F32),
        ],
        compiler_params=_cparams(("parallel", "parallel", "arbitrary")),
    )(qkvz, qkvz, qkvz, qkvz, lamv, subw_col)


def _gla_kernel(q_ref, k_ref, v_ref, z_ref, g_ref, wg_ref, bg_ref, tri_ref, nw_ref,
                o_ref, st_scr):
    t = pl.program_id(2)

    @pl.when(t == 0)
    def _():
        st_scr[...] = jnp.zeros(st_scr.shape, F32)

    pre = jnp.dot(g_ref[...].astype(BF16), wg_ref[...], preferred_element_type=F32) + bg_ref[...]
    log_a = (jnp.minimum(pre, 0.0) - jnp.log1p(jnp.exp(-jnp.abs(pre)))) * (1.0 / GLA_GATE_TAU)
    hi = log_a.astype(BF16)
    lo = (log_a - hi.astype(F32)).astype(BF16)
    tri = tri_ref[...]
    bcum = (jnp.dot(tri, hi, preferred_element_type=F32)
            + jnp.dot(tri, lo, preferred_element_type=F32))

    row = lax.broadcasted_iota(jnp.int32, (GLA_CHUNK, GLA_CHUNK), 0)
    col = lax.broadcasted_iota(jnp.int32, (GLA_CHUNK, GLA_CHUNK), 1)
    causal = col <= row
    nw = nw_ref[...]

    for c in range(GLA_ROWS // GLA_CHUNK):
        sl = slice(c * GLA_CHUNK, (c + 1) * GLA_CHUNK)
        b = bcum[sl, :]
        b_last = b[GLA_CHUNK - 1:GLA_CHUNK, :]
        qc = q_ref[sl, :].astype(F32) * (GLA_HEAD_K ** -0.5)
        kc = k_ref[sl, :].astype(F32)
        vc = v_ref[sl, :]
        q_dec = (qc * jnp.exp(b)).astype(BF16)
        k_inv = (kc * jnp.exp(-b)).astype(BF16)
        k_end = (kc * jnp.exp(b_last - b)).astype(BF16)
        attn = lax.dot_general(q_dec, k_inv, (((1,), (1,)), ((), ())),
                               preferred_element_type=F32)
        attn = jnp.where(causal, attn, 0.0).astype(BF16)
        st = st_scr[...]
        o = (jnp.dot(attn, vc, preferred_element_type=F32)
             + lax.dot_general(q_dec, st.astype(BF16), (((1,), (1,)), ((), ())),
                               preferred_element_type=F32))
        st_scr[...] = st * jnp.exp(b_last) + lax.dot_general(
            vc, k_end, (((0,), (0,)), ((), ())), preferred_element_type=F32)
        ms = jnp.mean(o * o, axis=-1, keepdims=True)
        zc = z_ref[sl, :].astype(F32)
        o_ref[sl, :] = (o * lax.rsqrt(ms + EPS) * nw * (zc * jax.nn.sigmoid(zc))).astype(BF16)


def _gla(main, glow, wg_bf, bg, tri, nw):
    nt = SEQ // GLA_ROWS
    kq = GLA_DK // GLA_HEAD_K
    voff = 2 * GLA_DK // GLA_HEAD_V
    zoff = voff + D_INNER // GLA_HEAD_V
    return pl.pallas_call(
        _gla_kernel,
        grid=(BATCH, GLA_HEADS, nt),
        in_specs=[
            pl.BlockSpec((GLA_ROWS, GLA_HEAD_K), lambda b, h, t: (b * nt + t, h)),
            pl.BlockSpec((GLA_ROWS, GLA_HEAD_K), lambda b, h, t: (b * nt + t, kq + h)),
            pl.BlockSpec((GLA_ROWS, GLA_HEAD_V), lambda b, h, t: (b * nt + t, voff + h)),
            pl.BlockSpec((GLA_ROWS, GLA_HEAD_V), lambda b, h, t: (b * nt + t, zoff + h)),
            pl.BlockSpec((GLA_ROWS, LANES), lambda b, h, t: (b * nt + t, 0)),
            pl.BlockSpec((LANES, GLA_HEAD_K), lambda b, h, t: (0, h)),
            pl.BlockSpec((1, GLA_HEAD_K), lambda b, h, t: (0, h)),
            pl.BlockSpec((GLA_ROWS, GLA_ROWS), lambda b, h, t: (0, 0)),
            pl.BlockSpec((1, GLA_HEAD_V), lambda b, h, t: (0, 0)),
        ],
        out_specs=pl.BlockSpec((GLA_ROWS, GLA_HEAD_V), lambda b, h, t: (b * nt + t, h)),
        out_shape=jax.ShapeDtypeStruct((M_ROWS, D_INNER), BF16),
        scratch_shapes=[pltpu.VMEM((GLA_HEAD_V, GLA_HEAD_K), F32)],
        compiler_params=_cparams(("parallel", "parallel", "arbitrary")),
    )(main, main, main, main, glow, wg_bf, bg, tri, nw)


def kernel(x, c, positions, ada_w, ada_b, norm_g, da_w_in, da_lam_q1, da_lam_k1, da_lam_q2,
           da_lam_k2, da_subln_w, da_w_out, gla_w_in, gla_w_gate_up, gla_b_gate, gla_norm_w,
           gla_w_out, final_g):
    x2 = x.reshape(M_ROWS, D_MODEL)
    c_pad = jnp.pad(c, ((0, ADA_ROWS - BATCH), (0, 0))).astype(BF16)
    mod = _ada(c_pad, ada_w, ada_b.reshape(DEPTH, 1, 3 * D_MODEL))
    mod = mod.reshape(DEPTH * ADA_ROWS * 3, 1, D_MODEL)
    norm_g3 = norm_g.reshape(DEPTH, 1, D_MODEL)

    inv_freq = ROPE_THETA ** (-jnp.arange(0, DA_HEAD_DIM, 2, dtype=F32) / DA_HEAD_DIM)
    cos, sin = _rope_tables(positions.reshape(BATCH, 1, SEQ), inv_freq.reshape(ROPE_HALF, 1))

    lambda_init = 0.8 - 0.6 * math.exp(-0.3 * 0)
    qkvz = _inproj_da(x2, norm_g3, mod, da_w_in[0].astype(BF16), cos, sin, layer=0)
    lamv = jnp.concatenate([da_lam_q1, da_lam_k1, da_lam_q2, da_lam_k2], axis=0)
    lamv = jnp.pad(lamv, ((0, 4), (0, LANES - DA_HEAD_DIM)))
    o = _diff_attn(qkvz, lamv, da_subln_w[0].reshape(DA_V_DIM, 1), lambda_init)
    x2 = _outproj(o, da_w_out[0].astype(BF16), x2, mod, final_g.reshape(1, D_MODEL),
                  layer=0, final_norm=False)

    w = gla_w_in[0]
    wg_in = jnp.pad(w[:, GLA_MAIN:], ((0, 0), (0, LANES - GLA_GATE_RANK))).astype(BF16)
    main, glow = _inproj_gla(x2, norm_g3, mod, w[:, :GLA_MAIN].astype(BF16), wg_in, layer=1)
    wg_up = jnp.pad(gla_w_gate_up[0], ((0, LANES - GLA_GATE_RANK), (0, 0))).astype(BF16)
    r = jnp.arange(GLA_ROWS)
    tri = ((r[:, None] >= r[None, :])
           & (r[:, None] // GLA_CHUNK == r[None, :] // GLA_CHUNK)).astype(BF16)
    o = _gla(main, glow, wg_up, gla_b_gate, tri, gla_norm_w)
    out = _outproj(o, gla_w_out[0].astype(BF16), x2, mod, final_g.reshape(1, D_MODEL),
                   layer=1, final_norm=True)
    return out.reshape(BATCH, SEQ, D_MODEL)
```

```python
import functools
import math

import jax
import jax.numpy as jnp
from jax import lax
from jax.experimental import pallas as pl
from jax.experimental.pallas import tpu as pltpu

F32 = jnp.float32
BF16 = jnp.bfloat16

D_MODEL = 1024
BATCH = 4
SEQ = 4096
DEPTH = 2
D_INNER = 2 * D_MODEL
EPS = 1e-6
M_ROWS = BATCH * SEQ

DA_HEADS = 16
DA_HEAD_DIM = 64
DA_V_DIM = 128
ROPE_THETA = 10000.0
ROPE_HALF = DA_HEAD_DIM // 2

GLA_HEADS = 4
GLA_DK = 1024
GLA_HEAD_K = 256
GLA_HEAD_V = 512
GLA_GATE_RANK = 16
GLA_GATE_TAU = 16.0
GLA_CHUNK = 64
GLA_MAIN = 2 * GLA_DK + 2 * D_INNER

LANES = 128
ADA_ROWS = 16
VMEM_LIMIT = 56 * 1024 * 1024

TM_IN, TN_IN = 1024, 1024
TM_OUT = 512
TQ = 512
TK = 256
CB = 256
assert TQ == 2 * TK and (2 * TQ) % CB == 0 and TQ % CB == 0
ONES_ROWS = 16
GLA_ROWS = 256
TM_ROPE = 1024
TN_ADA = 1536

NEG = -1e30


def _cparams(sem):
    return pltpu.CompilerParams(dimension_semantics=sem, vmem_limit_bytes=VMEM_LIMIT)


def _ada_kernel(c_ref, w_ref, b_ref, o_ref):
    o_ref[...] = jnp.dot(c_ref[...], w_ref[...].astype(BF16),
                         preferred_element_type=F32) + b_ref[...]


def _ada(c_pad, ada_w, ada_b3):
    n = 3 * D_MODEL
    return pl.pallas_call(
        _ada_kernel,
        grid=(DEPTH, n // TN_ADA),
        in_specs=[
            pl.BlockSpec((ADA_ROWS, D_MODEL), lambda l, j: (0, 0)),
            pl.BlockSpec((None, D_MODEL, TN_ADA), lambda l, j: (l, 0, j)),
            pl.BlockSpec((None, 1, TN_ADA), lambda l, j: (l, 0, j)),
        ],
        out_specs=pl.BlockSpec((None, ADA_ROWS, TN_ADA), lambda l, j: (l, 0, j)),
        out_shape=jax.ShapeDtypeStruct((DEPTH, ADA_ROWS, n), F32),
        compiler_params=_cparams(("parallel", "parallel")),
    )(c_pad, ada_w, ada_b3)


def _rope_kernel(pos_ref, invf_ref, cos_ref, sin_ref):
    ang = invf_ref[...] * pos_ref[...].astype(F32)
    c = jnp.cos(ang)
    s = jnp.sin(ang)
    cos_ref[...] = jnp.concatenate([c, c, c, c], axis=0).T
    sin_ref[...] = jnp.concatenate([-s, s, -s, s], axis=0).T


def _rope_tables(pos3, invf):
    nt = SEQ // TM_ROPE
    return pl.pallas_call(
        _rope_kernel,
        grid=(BATCH, nt),
        in_specs=[
            pl.BlockSpec((None, 1, TM_ROPE), lambda b, t: (b, 0, t)),
            pl.BlockSpec((ROPE_HALF, 1), lambda b, t: (0, 0)),
        ],
        out_specs=[
            pl.BlockSpec((TM_ROPE, LANES), lambda b, t: (b * nt + t, 0)),
            pl.BlockSpec((TM_ROPE, LANES), lambda b, t: (b * nt + t, 0)),
        ],
        out_shape=[jax.ShapeDtypeStruct((M_ROWS, LANES), F32)] * 2,
        compiler_params=_cparams(("parallel", "parallel")),
    )(pos3, invf)


def _norm_modulate(x, g, scale, shift):
    ms = jnp.mean(x * x, axis=-1, keepdims=True)
    return x * lax.rsqrt(ms + EPS) * (g * (1.0 + scale)) + shift


def _inproj_da_kernel(x_ref, g_ref, scale_ref, shift_ref, w_ref, cos_ref, sin_ref,
                      o_ref, h_scr, *, n_q_tiles, n_rope_tiles, q_scale):
    j = pl.program_id(1)

    @pl.when(j == 0)
    def _():
        h_scr[...] = _norm_modulate(x_ref[...], g_ref[...], scale_ref[...],
                                    shift_ref[...]).astype(BF16)

    acc = jnp.dot(h_scr[...], w_ref[...], preferred_element_type=F32)

    @pl.when(j < n_rope_tiles)
    def _():
        sc = jnp.where(j < n_q_tiles, q_scale, 1.0).astype(F32)
        cos = cos_ref[...] * sc
        sin = sin_ref[...] * sc
        lane = lax.broadcasted_iota(jnp.int32, cos.shape, 1)
        first_half = (lane % DA_HEAD_DIM) < ROPE_HALF
        for gidx in range(acc.shape[1] // LANES):
            t = acc[:, gidx * LANES:(gidx + 1) * LANES]
            rot = jnp.where(first_half, pltpu.roll(t, LANES - ROPE_HALF, 1),
                            pltpu.roll(t, ROPE_HALF, 1))
            o_ref[:, gidx * LANES:(gidx + 1) * LANES] = (t * cos + rot * sin).astype(BF16)

    @pl.when(j >= n_rope_tiles)
    def _():
        o_ref[...] = acc.astype(BF16)


def _inproj_da(x2, norm_g, mod, w_bf, cos, sin, layer):
    n = w_bf.shape[1]
    seg_tiles = D_INNER // TN_IN
    tiles_per_batch = SEQ // TM_IN

    def mod_idx(which):
        return lambda i, j: ((layer * ADA_ROWS + i // tiles_per_batch) * 3 + which, 0, 0)

    kern = functools.partial(_inproj_da_kernel, n_q_tiles=seg_tiles,
                             n_rope_tiles=2 * seg_tiles,
                             q_scale=DA_HEAD_DIM ** -0.5 * math.log2(math.e))
    return pl.pallas_call(
        kern,
        grid=(M_ROWS // TM_IN, n // TN_IN),
        in_specs=[
            pl.BlockSpec((TM_IN, D_MODEL), lambda i, j: (i, 0)),
            pl.BlockSpec((None, 1, D_MODEL), lambda i, j: (layer, 0, 0)),
            pl.BlockSpec((None, 1, D_MODEL), mod_idx(1)),
            pl.BlockSpec((None, 1, D_MODEL), mod_idx(0)),
            pl.BlockSpec((D_MODEL, TN_IN), lambda i, j: (0, j)),
            pl.BlockSpec((TM_IN, LANES), lambda i, j: (i, 0)),
            pl.BlockSpec((TM_IN, LANES), lambda i, j: (i, 0)),
        ],
        out_specs=pl.BlockSpec((None, TM_IN, TN_IN),
                               lambda i, j: (j // seg_tiles, i, j % seg_tiles)),
        out_shape=jax.ShapeDtypeStruct((4, M_ROWS, D_INNER), BF16),
        scratch_shapes=[pltpu.VMEM((TM_IN, D_MODEL), BF16)],
        compiler_params=_cparams(("parallel", "arbitrary")),
    )(x2, norm_g, mod, mod, w_bf, cos, sin)


def _inproj_gla_kernel(x_ref, g_ref, scale_ref, shift_ref, w_ref, wg_ref,
                       o_ref, glow_ref, h_scr):
    j = pl.program_id(1)

    @pl.when(j == 0)
    def _():
        h = _norm_modulate(x_ref[...], g_ref[...], scale_ref[...], shift_ref[...]).astype(BF16)
        h_scr[...] = h
        glow_ref[...] = jnp.dot(h, wg_ref[...], preferred_element_type=F32)

    o_ref[...] = jnp.dot(h_scr[...], w_ref[...], preferred_element_type=F32).astype(BF16)


def _inproj_gla(x2, norm_g, mod, w_bf, wg_bf, layer):
    n = w_bf.shape[1]
    tiles_per_batch = SEQ // TM_IN

    def mod_idx(which):
        return lambda i, j: ((layer * ADA_ROWS + i // tiles_per_batch) * 3 + which, 0, 0)

    return pl.pallas_call(
        _inproj_gla_kernel,
        grid=(M_ROWS // TM_IN, n // TN_IN),
        in_specs=[
            pl.BlockSpec((TM_IN, D_MODEL), lambda i, j: (i, 0)),
            pl.BlockSpec((None, 1, D_MODEL), lambda i, j: (layer, 0, 0)),
            pl.BlockSpec((None, 1, D_MODEL), mod_idx(1)),
            pl.BlockSpec((None, 1, D_MODEL), mod_idx(0)),
            pl.BlockSpec((D_MODEL, TN_IN), lambda i, j: (0, j)),
            pl.BlockSpec((D_MODEL, LANES), lambda i, j: (0, 0)),
        ],
        out_specs=[
            pl.BlockSpec((TM_IN, TN_IN), lambda i, j: (i, j)),
            pl.BlockSpec((TM_IN, LANES), lambda i, j: (i, 0)),
        ],
        out_shape=[jax.ShapeDtypeStruct((M_ROWS, n), BF16),
                   jax.ShapeDtypeStruct((M_ROWS, LANES), F32)],
        scratch_shapes=[pltpu.VMEM((TM_IN, D_MODEL), BF16)],
        compiler_params=_cparams(("parallel", "arbitrary")),
    )(x2, norm_g, mod, mod, w_bf, wg_bf)


def _outproj_kernel(o_ref, w_ref, x_ref, gate_ref, fg_ref, out_ref, *, final_norm):
    y = jnp.dot(o_ref[...], w_ref[...], preferred_element_type=F32)
    x = x_ref[...] + gate_ref[...] * y
    if final_norm:
        ms = jnp.mean(x * x, axis=-1, keepdims=True)
        x = x * lax.rsqrt(ms + EPS) * fg_ref[...]
    out_ref[...] = x


def _outproj(o, w_bf, x2, mod, final_g, layer, final_norm):
    tiles_per_batch = SEQ // TM_OUT
    kern = functools.partial(_outproj_kernel, final_norm=final_norm)
    return pl.pallas_call(
        kern,
        grid=(M_ROWS // TM_OUT,),
        in_specs=[
            pl.BlockSpec((TM_OUT, D_INNER), lambda i: (i, 0)),
            pl.BlockSpec((D_INNER, D_MODEL), lambda i: (0, 0)),
            pl.BlockSpec((TM_OUT, D_MODEL), lambda i: (i, 0)),
            pl.BlockSpec((None, 1, D_MODEL),
                         lambda i: ((layer * ADA_ROWS + i // tiles_per_batch) * 3 + 2, 0, 0)),
            pl.BlockSpec((1, D_MODEL), lambda i: (0, 0)),
        ],
        out_specs=pl.BlockSpec((TM_OUT, D_MODEL), lambda i: (i, 0)),
        out_shape=jax.ShapeDtypeStruct((M_ROWS, D_MODEL), F32),
        compiler_params=_cparams(("parallel",)),
    )(o, w_bf, x2, mod, final_g)


def _diff_attn_kernel(q_ref, k_ref, v_ref, z_ref, lamv_ref, subw_ref, o_ref,
                      vt_scr, m_scr, acc_scr, p_scr, alpha_scr, *, lambda_init):
    qi = pl.program_id(2)

    @pl.when(qi == 0)
    def _():
        vt_scr[0:DA_V_DIM, :] = v_ref[...].astype(F32).T.astype(BF16)
        vt_scr[DA_V_DIM:, :] = jnp.ones((ONES_ROWS, SEQ), BF16)

    q = q_ref[...]
    lane = lax.broadcasted_iota(jnp.int32, q.shape, 1)
    zero = jnp.zeros_like(q)
    qs = jnp.concatenate([jnp.where(lane < DA_HEAD_DIM, q, zero),
                          jnp.where(lane >= DA_HEAD_DIM, q, zero)], axis=0)

    m_scr[...] = jnp.full(m_scr.shape, NEG, F32)
    acc_scr[...] = jnp.zeros(acc_scr.shape, F32)

    def deferred_pv(prev_start, cs):
        vt = vt_scr[:, pl.ds(pl.multiple_of(prev_start, TK), TK)]
        acc_scr[:, cs] = alpha_scr[:, cs] * acc_scr[:, cs] + jnp.dot(
            vt, p_scr[:, cs], preferred_element_type=F32)

    def key_round(start, diag_offset, prev_start):
        k = k_ref[pl.ds(pl.multiple_of(start, TK), TK), :]
        for c in range(2 * TQ // CB):
            cs = slice(c * CB, (c + 1) * CB)
            q_lo = (c * CB) % TQ
            masked_out = diag_offset is not None and diag_offset > q_lo + CB - 1
            needs_mask = (diag_offset is not None and not masked_out
                          and diag_offset + TK - 1 > q_lo)
            if not masked_out:
                st = lax.dot_general(k, qs[cs], (((1,), (1,)), ((), ())),
                                     preferred_element_type=F32)
            if prev_start is not None:
                deferred_pv(prev_start, cs)
            if masked_out:
                alpha_scr[:, cs] = jnp.ones((1, CB), F32)
                p_scr[:, cs] = jnp.zeros((TK, CB), BF16)
                continue
            if needs_mask:
                kpos = lax.broadcasted_iota(jnp.int32, st.shape, 0) + diag_offset
                qpos = lax.broadcasted_iota(jnp.int32, st.shape, 1) + q_lo
                st = jnp.where(kpos <= qpos, st, NEG)
            m_prev = m_scr[:, cs]
            m_new = jnp.maximum(m_prev, jnp.max(st, axis=0, keepdims=True))
            alpha_scr[:, cs] = jnp.exp2(m_prev - m_new)
            p_scr[:, cs] = jnp.exp2(st - m_new).astype(BF16)
            m_scr[:, cs] = m_new

    diag = qi * TQ
    key_round(diag, 0, None)
    key_round(diag + TK, TK, diag)

    def body(kc, carry):
        base = kc * TQ
        key_round(base, None, jnp.where(kc == 0, diag + TK, base - TK))
        key_round(base + TK, None, base)
        return carry

    lax.fori_loop(0, qi, body, 0)
    last_start = jnp.where(qi == 0, TK, diag - TK)
    for c in range(2 * TQ // CB):
        deferred_pv(last_start, slice(c * CB, (c + 1) * CB))

    lamv = lamv_ref[...]
    lam = (jnp.exp(jnp.sum(lamv[0:1, :] * lamv[1:2, :], keepdims=True))
           - jnp.exp(jnp.sum(lamv[2:3, :] * lamv[3:4, :], keepdims=True)) + lambda_init)
    acc = acc_scr[...]
    on = acc[0:DA_V_DIM, :] / acc[DA_V_DIM:DA_V_DIM + 1, :]
    ot = on[:, :TQ] - lam * on[:, TQ:]
    ms = jnp.mean(ot * ot, axis=0, keepdims=True)
    ot = ot * lax.rsqrt(ms + EPS) * (subw_ref[...] * (1.0 - lambda_init))
    z = z_ref[...].astype(F32)
    o_ref[...] = (ot.T * (z * jax.nn.sigmoid(z))).astype(BF16)


def _diff_attn(qkvz, lamv, subw_col, lambda_init):
    nq = SEQ // TQ
    kern = functools.partial(_diff_attn_kernel, lambda_init=lambda_init)
    return pl.pallas_call(
        kern,
        grid=(BATCH, DA_HEADS, nq),
        in_specs=[
            pl.BlockSpec((None, TQ, LANES), lambda b, h, i: (0, b * nq + i, h)),
            pl.BlockSpec((None, SEQ, LANES), lambda b, h, i: (1, b, h)),
            pl.BlockSpec((None, SEQ, LANES), lambda b, h, i: (2, b, h)),
            pl.BlockSpec((None, TQ, LANES), lambda b, h, i: (3, b * nq + i, h)),
            pl.BlockSpec((8, LANES), lambda b, h, i: (0, 0)),
            pl.BlockSpec((DA_V_DIM, 1), lambda b, h, i: (0, 0)),
        ],
        out_specs=pl.BlockSpec((TQ, LANES), lambda b, h, i: (b * nq + i, h)),
        out_shape=jax.ShapeDtypeStruct((M_ROWS, D_INNER), BF16),
        scratch_shapes=[
            pltpu.VMEM((DA_V_DIM + ONES_ROWS, SEQ), BF16),
            pltpu.VMEM((1, 2 * TQ), F32),
            pltpu.VMEM((DA_V_DIM + ONES_ROWS, 2 * TQ), F32),
            pltpu.VMEM((TK, 2 * TQ), BF16),
            pltpu.VMEM((1, 2 * TQ), F32),
        ],
        compiler_params=_cparams(("parallel", "parallel", "arbitrary")),
    )(qkvz, qkvz, qkvz, qkvz, lamv, subw_col)


def _gla_kernel(q_ref, k_ref, v_ref, z_ref, g_ref, wg_ref, bg_ref, tri_ref, nw_ref,
                o_ref, st_scr):
    t = pl.program_id(2)

    @pl.when(t == 0)
    def _():
        st_scr[...] = jnp.zeros(st_scr.shape, F32)

    pre = jnp.dot(g_ref[...].astype(BF16), wg_ref[...], preferred_element_type=F32) + bg_ref[...]
    log_a = (jnp.minimum(pre, 0.0) - jnp.log1p(jnp.exp(-jnp.abs(pre)))) * (1.0 / GLA_GATE_TAU)
    hi = log_a.astype(BF16)
    lo = (log_a - hi.astype(F32)).astype(BF16)
    tri = tri_ref[...]
    bcum = (jnp.dot(tri, hi, preferred_element_type=F32)
            + jnp.dot(tri, lo, preferred_element_type=F32))

    row = lax.broadcasted_iota(jnp.int32, (GLA_CHUNK, GLA_CHUNK), 0)
    col = lax.broadcasted_iota(jnp.int32, (GLA_CHUNK, GLA_CHUNK), 1)
    causal = col <= row
    nw = nw_ref[...]

    for c in range(GLA_ROWS // GLA_CHUNK):
        sl = slice(c * GLA_CHUNK, (c + 1) * GLA_CHUNK)
        b = bcum[sl, :]
        b_last = b[GLA_CHUNK - 1:GLA_CHUNK, :]
        qc = q_ref[sl, :].astype(F32) * (GLA_HEAD_K ** -0.5)
        kc = k_ref[sl, :].astype(F32)
        vc = v_ref[sl, :]
        q_dec = (qc * jnp.exp(b)).astype(BF16)
        k_inv = (kc * jnp.exp(-b)).astype(BF16)
        k_end = (kc * jnp.exp(b_last - b)).astype(BF16)
        attn = lax.dot_general(q_dec, k_inv, (((1,), (1,)), ((), ())),
                               preferred_element_type=F32)
        attn = jnp.where(causal, attn, 0.0).astype(BF16)
        st = st_scr[...]
        o = (jnp.dot(attn, vc, preferred_element_type=F32)
             + lax.dot_general(q_dec, st.astype(BF16), (((1,), (1,)), ((), ())),
                               preferred_element_type=F32))
        st_scr[...] = st * jnp.exp(b_last) + lax.dot_general(
            vc, k_end, (((0,), (0,)), ((), ())), preferred_element_type=F32)
        ms = jnp.mean(o * o, axis=-1, keepdims=True)
        zc = z_ref[sl, :].astype(F32)
        o_ref[sl, :] = (o * lax.rsqrt(ms + EPS) * nw * (zc * jax.nn.sigmoid(zc))).astype(BF16)


def _gla(main, glow, wg_bf, bg, tri, nw):
    nt = SEQ // GLA_ROWS
    kq = GLA_DK // GLA_HEAD_K
    voff = 2 * GLA_DK // GLA_HEAD_V
    zoff = voff + D_INNER // GLA_HEAD_V
    return pl.pallas_call(
        _gla_kernel,
        grid=(BATCH, GLA_HEADS, nt),
        in_specs=[
            pl.BlockSpec((GLA_ROWS, GLA_HEAD_K), lambda b, h, t: (b * nt + t, h)),
            pl.BlockSpec((GLA_ROWS, GLA_HEAD_K), lambda b, h, t: (b * nt + t, kq + h)),
            pl.BlockSpec((GLA_ROWS, GLA_HEAD_V), lambda b, h, t: (b * nt + t, voff + h)),
            pl.BlockSpec((GLA_ROWS, GLA_HEAD_V), lambda b, h, t: (b * nt + t, zoff + h)),
            pl.BlockSpec((GLA_ROWS, LANES), lambda b, h, t: (b * nt + t, 0)),
            pl.BlockSpec((LANES, GLA_HEAD_K), lambda b, h, t: (0, h)),
            pl.BlockSpec((1, GLA_HEAD_K), lambda b, h, t: (0, h)),
            pl.BlockSpec((GLA_ROWS, GLA_ROWS), lambda b, h, t: (0, 0)),
            pl.BlockSpec((1, GLA_HEAD_V), lambda b, h, t: (0, 0)),
        ],
        out_specs=pl.BlockSpec((GLA_ROWS, GLA_HEAD_V), lambda b, h, t: (b * nt + t, h)),
        out_shape=jax.ShapeDtypeStruct((M_ROWS, D_INNER), BF16),
        scratch_shapes=[pltpu.VMEM((GLA_HEAD_V, GLA_HEAD_K), F32)],
        compiler_params=_cparams(("parallel", "parallel", "arbitrary")),
    )(main, main, main, main, glow, wg_bf, bg, tri, nw)


def kernel(x, c, positions, ada_w, ada_b, norm_g, da_w_in, da_lam_q1, da_lam_k1, da_lam_q2,
           da_lam_k2, da_subln_w, da_w_out, gla_w_in, gla_w_gate_up, gla_b_gate, gla_norm_w,
           gla_w_out, final_g):
    x2 = x.reshape(M_ROWS, D_MODEL)
    c_pad = jnp.pad(c, ((0, ADA_ROWS - BATCH), (0, 0))).astype(BF16)
    mod = _ada(c_pad, ada_w, ada_b.reshape(DEPTH, 1, 3 * D_MODEL))
    mod = mod.reshape(DEPTH * ADA_ROWS * 3, 1, D_MODEL)
    norm_g3 = norm_g.reshape(DEPTH, 1, D_MODEL)

    inv_freq = ROPE_THETA ** (-jnp.arange(0, DA_HEAD_DIM, 2, dtype=F32) / DA_HEAD_DIM)
    cos, sin = _rope_tables(positions.reshape(BATCH, 1, SEQ), inv_freq.reshape(ROPE_HALF, 1))

    lambda_init = 0.8 - 0.6 * math.exp(-0.3 * 0)
    qkvz = _inproj_da(x2, norm_g3, mod, da_w_in[0].astype(BF16), cos, sin, layer=0)
    lamv = jnp.concatenate([da_lam_q1, da_lam_k1, da_lam_q2, da_lam_k2], axis=0)
    lamv = jnp.pad(lamv, ((0, 4), (0, LANES - DA_HEAD_DIM)))
    o = _diff_attn(qkvz, lamv, da_subln_w[0].reshape(DA_V_DIM, 1), lambda_init)
    x2 = _outproj(o, da_w_out[0].astype(BF16), x2, mod, final_g.reshape(1, D_MODEL),
                  layer=0, final_norm=False)

    w = gla_w_in[0]
    wg_in = jnp.pad(w[:, GLA_MAIN:], ((0, 0), (0, LANES - GLA_GATE_RANK))).astype(BF16)
    main, glow = _inproj_gla(x2, norm_g3, mod, w[:, :GLA_MAIN].astype(BF16), wg_in, layer=1)
    wg_up = jnp.pad(gla_w_gate_up[0], ((0, LANES - GLA_GATE_RANK), (0, 0))).astype(BF16)
    r = jnp.arange(GLA_ROWS)
    tri = ((r[:, None] >= r[None, :])
           & (r[:, None] // GLA_CHUNK == r[None, :] // GLA_CHUNK)).astype(BF16)
    o = _gla(main, glow, wg_up, gla_b_gate, tri, gla_norm_w)
    out = _outproj(o, gla_w_out[0].astype(BF16), x2, mod, final_g.reshape(1, D_MODEL),
                   layer=1, final_norm=True)
    return out.reshape(BATCH, SEQ, D_MODEL)
```

```python
import functools
import math

import jax
import jax.numpy as jnp
from jax import lax
from jax.experimental import pallas as pl
from jax.experimental.pallas import tpu as pltpu

F32 = jnp.float32
BF16 = jnp.bfloat16

D_MODEL = 1024
BATCH = 4
SEQ = 4096
DEPTH = 2
D_INNER = 2 * D_MODEL
EPS = 1e-6
M_ROWS = BATCH * SEQ

DA_HEADS = 16
DA_HEAD_DIM = 64
DA_V_DIM = 128
ROPE_THETA = 10000.0
ROPE_HALF = DA_HEAD_DIM // 2

GLA_HEADS = 4
GLA_DK = 1024
GLA_HEAD_K = 256
GLA_HEAD_V = 512
GLA_GATE_RANK = 16
GLA_GATE_TAU = 16.0
GLA_CHUNK = 64
GLA_MAIN = 2 * GLA_DK + 2 * D_INNER

LANES = 128
ADA_ROWS = 16
VMEM_LIMIT = 56 * 1024 * 1024

TM_IN, TN_IN = 1024, 1024
TM_OUT = 512
TQ = 1024
TK = 256
NSUB = TQ // TK
CB = 256
assert TQ % TK == 0 and TQ % CB == 0
ONES_ROWS = 16
GLA_ROWS = 256
TM_ROPE = 1024
TN_ADA = 1536

NEG = -1e30


def _cparams(sem):
    return pltpu.CompilerParams(dimension_semantics=sem, vmem_limit_bytes=VMEM_LIMIT)


def _ada_kernel(c_ref, w_ref, b_ref, o_ref):
    o_ref[...] = jnp.dot(c_ref[...], w_ref[...].astype(BF16),
                         preferred_element_type=F32) + b_ref[...]


def _ada(c_pad, ada_w, ada_b3):
    n = 3 * D_MODEL
    return pl.pallas_call(
        _ada_kernel,
        grid=(DEPTH, n // TN_ADA),
        in_specs=[
            pl.BlockSpec((ADA_ROWS, D_MODEL), lambda l, j: (0, 0)),
            pl.BlockSpec((None, D_MODEL, TN_ADA), lambda l, j: (l, 0, j)),
            pl.BlockSpec((None, 1, TN_ADA), lambda l, j: (l, 0, j)),
        ],
        out_specs=pl.BlockSpec((None, ADA_ROWS, TN_ADA), lambda l, j: (l, 0, j)),
        out_shape=jax.ShapeDtypeStruct((DEPTH, ADA_ROWS, n), F32),
        compiler_params=_cparams(("parallel", "parallel")),
    )(c_pad, ada_w, ada_b3)


def _rope_kernel(pos_ref, invf_ref, cos_ref, sin_ref):
    ang = invf_ref[...] * pos_ref[...].astype(F32)
    c = jnp.cos(ang)
    s = jnp.sin(ang)
    cos_ref[...] = jnp.concatenate([c, c, c, c], axis=0).T
    sin_ref[...] = jnp.concatenate([-s, s, -s, s], axis=0).T


def _rope_tables(pos3, invf):
    nt = SEQ // TM_ROPE
    return pl.pallas_call(
        _rope_kernel,
        grid=(BATCH, nt),
        in_specs=[
            pl.BlockSpec((None, 1, TM_ROPE), lambda b, t: (b, 0, t)),
            pl.BlockSpec((ROPE_HALF, 1), lambda b, t: (0, 0)),
        ],
        out_specs=[
            pl.BlockSpec((TM_ROPE, LANES), lambda b, t: (b * nt + t, 0)),
            pl.BlockSpec((TM_ROPE, LANES), lambda b, t: (b * nt + t, 0)),
        ],
        out_shape=[jax.ShapeDtypeStruct((M_ROWS, LANES), F32)] * 2,
        compiler_params=_cparams(("parallel", "parallel")),
    )(pos3, invf)


def _norm_modulate(x, g, scale, shift):
    ms = jnp.mean(x * x, axis=-1, keepdims=True)
    return x * lax.rsqrt(ms + EPS) * (g * (1.0 + scale)) + shift


def _inproj_da_kernel(x_ref, g_ref, scale_ref, shift_ref, w_ref, cos_ref, sin_ref,
                      o_ref, h_scr, *, n_q_tiles, n_rope_tiles, q_scale):
    j = pl.program_id(1)

    @pl.when(j == 0)
    def _():
        h_scr[...] = _norm_modulate(x_ref[...], g_ref[...], scale_ref[...],
                                    shift_ref[...]).astype(BF16)

    acc = jnp.dot(h_scr[...], w_ref[...], preferred_element_type=F32)

    @pl.when(j < n_rope_tiles)
    def _():
        sc = jnp.where(j < n_q_tiles, q_scale, 1.0).astype(F32)
        cos = cos_ref[...] * sc
        sin = sin_ref[...] * sc
        lane = lax.broadcasted_iota(jnp.int32, cos.shape, 1)
        first_half = (lane % DA_HEAD_DIM) < ROPE_HALF
        for gidx in range(acc.shape[1] // LANES):
            t = acc[:, gidx * LANES:(gidx + 1) * LANES]
            rot = jnp.where(first_half, pltpu.roll(t, LANES - ROPE_HALF, 1),
                            pltpu.roll(t, ROPE_HALF, 1))
            o_ref[:, gidx * LANES:(gidx + 1) * LANES] = (t * cos + rot * sin).astype(BF16)

    @pl.when(j >= n_rope_tiles)
    def _():
        o_ref[...] = acc.astype(BF16)


def _inproj_da(x2, norm_g, mod, w_bf, cos, sin, layer):
    n = w_bf.shape[1]
    seg_tiles = D_INNER // TN_IN
    tiles_per_batch = SEQ // TM_IN

    def mod_idx(which):
        return lambda i, j: ((layer * ADA_ROWS + i // tiles_per_batch) * 3 + which, 0, 0)

    kern = functools.partial(_inproj_da_kernel, n_q_tiles=seg_tiles,
                             n_rope_tiles=2 * seg_tiles,
                             q_scale=DA_HEAD_DIM ** -0.5 * math.log2(math.e))
    return pl.pallas_call(
        kern,
        grid=(M_ROWS // TM_IN, n // TN_IN),
        in_specs=[
            pl.BlockSpec((TM_IN, D_MODEL), lambda i, j: (i, 0)),
            pl.BlockSpec((None, 1, D_MODEL), lambda i, j: (layer, 0, 0)),
            pl.BlockSpec((None, 1, D_MODEL), mod_idx(1)),
            pl.BlockSpec((None, 1, D_MODEL), mod_idx(0)),
            pl.BlockSpec((D_MODEL, TN_IN), lambda i, j: (0, j)),
            pl.BlockSpec((TM_IN, LANES), lambda i, j: (i, 0)),
            pl.BlockSpec((TM_IN, LANES), lambda i, j: (i, 0)),
        ],
        out_specs=pl.BlockSpec((None, TM_IN, TN_IN),
                               lambda i, j: (j // seg_tiles, i, j % seg_tiles)),
        out_shape=jax.ShapeDtypeStruct((4, M_ROWS, D_INNER), BF16),
        scratch_shapes=[pltpu.VMEM((TM_IN, D_MODEL), BF16)],
        compiler_params=_cparams(("parallel", "arbitrary")),
    )(x2, norm_g, mod, mod, w_bf, cos, sin)


def _inproj_gla_kernel(x_ref, g_ref, scale_ref, shift_ref, w_ref, wg_ref,
                       o_ref, glow_ref, h_scr):
    j = pl.program_id(1)

    @pl.when(j == 0)
    def _():
        h = _norm_modulate(x_ref[...], g_ref[...], scale_ref[...], shift_ref[...]).astype(BF16)
        h_scr[...] = h
        glow_ref[...] = jnp.dot(h, wg_ref[...], preferred_element_type=F32)

    o_ref[...] = jnp.dot(h_scr[...], w_ref[...], preferred_element_type=F32).astype(BF16)


def _inproj_gla(x2, norm_g, mod, w_bf, wg_bf, layer):
    n = w_bf.shape[1]
    tiles_per_batch = SEQ // TM_IN

    def mod_idx(which):
        return lambda i, j: ((layer * ADA_ROWS + i // tiles_per_batch) * 3 + which, 0, 0)

    return pl.pallas_call(
        _inproj_gla_kernel,
        grid=(M_ROWS // TM_IN, n // TN_IN),
        in_specs=[
            pl.BlockSpec((TM_IN, D_MODEL), lambda i, j: (i, 0)),
            pl.BlockSpec((None, 1, D_MODEL), lambda i, j: (layer, 0, 0)),
            pl.BlockSpec((None, 1, D_MODEL), mod_idx(1)),
            pl.BlockSpec((None, 1, D_MODEL), mod_idx(0)),
            pl.BlockSpec((D_MODEL, TN_IN), lambda i, j: (0, j)),
            pl.BlockSpec((D_MODEL, LANES), lambda i, j: (0, 0)),
        ],
        out_specs=[
            pl.BlockSpec((TM_IN, TN_IN), lambda i, j: (i, j)),
            pl.BlockSpec((TM_IN, LANES), lambda i, j: (i, 0)),
        ],
        out_shape=[jax.ShapeDtypeStruct((M_ROWS, n), BF16),
                   jax.ShapeDtypeStruct((M_ROWS, LANES), F32)],
        scratch_shapes=[pltpu.VMEM((TM_IN, D_MODEL), BF16)],
        compiler_params=_cparams(("parallel", "arbitrary")),
    )(x2, norm_g, mod, mod, w_bf, wg_bf)


def _outproj_kernel(o_ref, w_ref, x_ref, gate_ref, fg_ref, out_ref, *, final_norm):
    y = jnp.dot(o_ref[...], w_ref[...], preferred_element_type=F32)
    x = x_ref[...] + gate_ref[...] * y
    if final_norm:
        ms = jnp.mean(x * x, axis=-1, keepdims=True)
        x = x * lax.rsqrt(ms + EPS) * fg_ref[...]
    out_ref[...] = x


def _outproj(o, w_bf, x2, mod, final_g, layer, final_norm):
    tiles_per_batch = SEQ // TM_OUT
    kern = functools.partial(_outproj_kernel, final_norm=final_norm)
    return pl.pallas_call(
        kern,
        grid=(M_ROWS // TM_OUT,),
        in_specs=[
            pl.BlockSpec((TM_OUT, D_INNER), lambda i: (i, 0)),
            pl.BlockSpec((D_INNER, D_MODEL), lambda i: (0, 0)),
            pl.BlockSpec((TM_OUT, D_MODEL), lambda i: (i, 0)),
            pl.BlockSpec((None, 1, D_MODEL),
                         lambda i: ((layer * ADA_ROWS + i // tiles_per_batch) * 3 + 2, 0, 0)),
            pl.BlockSpec((1, D_MODEL), lambda i: (0, 0)),
        ],
        out_specs=pl.BlockSpec((TM_OUT, D_MODEL), lambda i: (i, 0)),
        out_shape=jax.ShapeDtypeStruct((M_ROWS, D_MODEL), F32),
        compiler_params=_cparams(("parallel",)),
    )(o, w_bf, x2, mod, final_g)


def _diff_attn_kernel(q_ref, k_ref, v_ref, z_ref, lamv_ref, subw_ref, o_ref,
                      vt_scr, m_scr, acc_scr, p_scr, alpha_scr, *, lambda_init):
    qi = pl.program_id(2)

    @pl.when(qi == 0)
    def _():
        vt_scr[0:DA_V_DIM, :] = v_ref[...].astype(F32).T.astype(BF16)
        vt_scr[DA_V_DIM:, :] = jnp.ones((ONES_ROWS, SEQ), BF16)

    q = q_ref[...]
    lane = lax.broadcasted_iota(jnp.int32, q.shape, 1)
    zero = jnp.zeros_like(q)
    qs = jnp.concatenate([jnp.where(lane < DA_HEAD_DIM, q, zero),
                          jnp.where(lane >= DA_HEAD_DIM, q, zero)], axis=0)

    m_scr[...] = jnp.full(m_scr.shape, NEG, F32)
    acc_scr[...] = jnp.zeros(acc_scr.shape, F32)

    def deferred_pv(prev_start, cs):
        vt = vt_scr[:, pl.ds(pl.multiple_of(prev_start, TK), TK)]
        acc_scr[:, cs] = alpha_scr[:, cs] * acc_scr[:, cs] + jnp.dot(
            vt, p_scr[:, cs], preferred_element_type=F32)

    def block_masked_out(diag_offset, c):
        return diag_offset is not None and diag_offset > (c * CB) % TQ + CB - 1

    def key_round(start, diag_offset, prev_start, prev_diag_offset=None):
        k = k_ref[pl.ds(pl.multiple_of(start, TK), TK), :]
        for c in range(2 * TQ // CB):
            cs = slice(c * CB, (c + 1) * CB)
            q_lo = (c * CB) % TQ
            masked_out = block_masked_out(diag_offset, c)
            needs_mask = (diag_offset is not None and not masked_out
                          and diag_offset + TK - 1 > q_lo)
            if not masked_out:
                st = lax.dot_general(k, qs[cs], (((1,), (1,)), ((), ())),
                                     preferred_element_type=F32)
            if prev_start is not None and not block_masked_out(prev_diag_offset, c):
                deferred_pv(prev_start, cs)
            if masked_out:
                alpha_scr[:, cs] = jnp.ones((1, CB), F32)
                p_scr[:, cs] = jnp.zeros((TK, CB), BF16)
                continue
            if needs_mask:
                kpos = lax.broadcasted_iota(jnp.int32, st.shape, 0) + diag_offset
                qpos = lax.broadcasted_iota(jnp.int32, st.shape, 1) + q_lo
                st = jnp.where(kpos <= qpos, st, NEG)
            m_prev = m_scr[:, cs]
            m_new = jnp.maximum(m_prev, jnp.max(st, axis=0, keepdims=True))
            alpha_scr[:, cs] = jnp.exp2(m_prev - m_new)
            p_scr[:, cs] = jnp.exp2(st - m_new).astype(BF16)
            m_scr[:, cs] = m_new

    diag = qi * TQ
    last_sub = (NSUB - 1) * TK
    key_round(diag, 0, None)
    for u in range(1, NSUB):
        key_round(diag + u * TK, u * TK, diag + (u - 1) * TK, (u - 1) * TK)

    def body(kc, carry):
        base = kc * TQ
        key_round(base, None, jnp.where(kc == 0, diag + last_sub, base - TK))
        for u in range(1, NSUB):
            key_round(base + u * TK, None, base + (u - 1) * TK)
        return carry

    lax.fori_loop(0, qi, body, 0)
    last_start = jnp.where(qi == 0, last_sub, diag - TK)
    for c in range(2 * TQ // CB):
        deferred_pv(last_start, slice(c * CB, (c + 1) * CB))

    lamv = lamv_ref[...]
    lam = (jnp.exp(jnp.sum(lamv[0:1, :] * lamv[1:2, :], keepdims=True))
           - jnp.exp(jnp.sum(lamv[2:3, :] * lamv[3:4, :], keepdims=True)) + lambda_init)
    acc = acc_scr[...]
    on = acc[0:DA_V_DIM, :] / acc[DA_V_DIM:DA_V_DIM + 1, :]
    ot = on[:, :TQ] - lam * on[:, TQ:]
    ms = jnp.mean(ot * ot, axis=0, keepdims=True)
    ot = ot * lax.rsqrt(ms + EPS) * (subw_ref[...] * (1.0 - lambda_init))
    z = z_ref[...].astype(F32)
    o_ref[...] = (ot.T * (z * jax.nn.sigmoid(z))).astype(BF16)


def _diff_attn(qkvz, lamv, subw_col, lambda_init):
    nq = SEQ // TQ
    kern = functools.partial(_diff_attn_kernel, lambda_init=lambda_init)
    return pl.pallas_call(
        kern,
        grid=(BATCH, DA_HEADS, nq),
        in_specs=[
            pl.BlockSpec((None, TQ, LANES), lambda b, h, i: (0, b * nq + i, h)),
            pl.BlockSpec((None, SEQ, LANES), lambda b, h, i: (1, b, h)),
            pl.BlockSpec((None, SEQ, LANES), lambda b, h, i: (2, b, h)),
            pl.BlockSpec((None, TQ, LANES), lambda b, h, i: (3, b * nq + i, h)),
            pl.BlockSpec((8, LANES), lambda b, h, i: (0, 0)),
            pl.BlockSpec((DA_V_DIM, 1), lambda b, h, i: (0, 0)),
        ],
        out_specs=pl.BlockSpec((TQ, LANES), lambda b, h, i: (b * nq + i, h)),
        out_shape=jax.ShapeDtypeStruct((M_ROWS, D_INNER), BF16),
        scratch_shapes=[
            pltpu.VMEM((DA_V_DIM + ONES_ROWS, SEQ), BF16),
            pltpu.VMEM((1, 2 * TQ), F32),
            pltpu.VMEM((DA_V_DIM + ONES_ROWS, 2 * TQ), F32),
            pltpu.VMEM((TK, 2 * TQ), BF16),
            pltpu.VMEM((1, 2 * TQ), F32),
        ],
        compiler_params=_cparams(("parallel", "parallel", "arbitrary")),
    )(qkvz, qkvz, qkvz, qkvz, lamv, subw_col)


def _gla_kernel(q_ref, k_ref, v_ref, z_ref, g_ref, wg_ref, bg_ref, tri_ref, nw_ref,
                o_ref, st_scr):
    t = pl.program_id(2)

    @pl.when(t == 0)
    def _():
        st_scr[...] = jnp.zeros(st_scr.shape, F32)

    pre = jnp.dot(g_ref[...].astype(BF16), wg_ref[...], preferred_element_type=F32) + bg_ref[...]
    log_a = (jnp.minimum(pre, 0.0) - jnp.log1p(jnp.exp(-jnp.abs(pre)))) * (1.0 / GLA_GATE_TAU)
    hi = log_a.astype(BF16)
    lo = (log_a - hi.astype(F32)).astype(BF16)
    tri = tri_ref[...]
    bcum = (jnp.dot(tri, hi, preferred_element_type=F32)
            + jnp.dot(tri, lo, preferred_element_type=F32))

    row = lax.broadcasted_iota(jnp.int32, (GLA_CHUNK, GLA_CHUNK), 0)
    col = lax.broadcasted_iota(jnp.int32, (GLA_CHUNK, GLA_CHUNK), 1)
    causal = col <= row
    nw = nw_ref[...]

    for c in range(GLA_ROWS // GLA_CHUNK):
        sl = slice(c * GLA_CHUNK, (c + 1) * GLA_CHUNK)
        b = bcum[sl, :]
        b_last = b[GLA_CHUNK - 1:GLA_CHUNK, :]
        qc = q_ref[sl, :].astype(F32) * (GLA_HEAD_K ** -0.5)
        kc = k_ref[sl, :].astype(F32)
        vc = v_ref[sl, :]
        q_dec = (qc * jnp.exp(b)).astype(BF16)
        k_inv = (kc * jnp.exp(-b)).astype(BF16)
        k_end = (kc * jnp.exp(b_last - b)).astype(BF16)
        attn = lax.dot_general(q_dec, k_inv, (((1,), (1,)), ((), ())),
                               preferred_element_type=F32)
        attn = jnp.where(causal, attn, 0.0).astype(BF16)
        st = st_scr[...]
        o = (jnp.dot(attn, vc, preferred_element_type=F32)
             + lax.dot_general(q_dec, st.astype(BF16), (((1,), (1,)), ((), ())),
                               preferred_element_type=F32))
        st_scr[...] = st * jnp.exp(b_last) + lax.dot_general(
            vc, k_end, (((0,), (0,)), ((), ())), preferred_element_type=F32)
        ms = jnp.mean(o * o, axis=-1, keepdims=True)
        zc = z_ref[sl, :].astype(F32)
        o_ref[sl, :] = (o * lax.rsqrt(ms + EPS) * nw * (zc * jax.nn.sigmoid(zc))).astype(BF16)


def _gla(main, glow, wg_bf, bg, tri, nw):
    nt = SEQ // GLA_ROWS
    kq = GLA_DK // GLA_HEAD_K
    voff = 2 * GLA_DK // GLA_HEAD_V
    zoff = voff + D_INNER // GLA_HEAD_V
    return pl.pallas_call(
        _gla_kernel,
        grid=(BATCH, GLA_HEADS, nt),
        in_specs=[
            pl.BlockSpec((GLA_ROWS, GLA_HEAD_K), lambda b, h, t: (b * nt + t, h)),
            pl.BlockSpec((GLA_ROWS, GLA_HEAD_K), lambda b, h, t: (b * nt + t, kq + h)),
            pl.BlockSpec((GLA_ROWS, GLA_HEAD_V), lambda b, h, t: (b * nt + t, voff + h)),
            pl.BlockSpec((GLA_ROWS, GLA_HEAD_V), lambda b, h, t: (b * nt + t, zoff + h)),
            pl.BlockSpec((GLA_ROWS, LANES), lambda b, h, t: (b * nt + t, 0)),
            pl.BlockSpec((LANES, GLA_HEAD_K), lambda b, h, t: (0, h)),
            pl.BlockSpec((1, GLA_HEAD_K), lambda b, h, t: (0, h)),
            pl.BlockSpec((GLA_ROWS, GLA_ROWS), lambda b, h, t: (0, 0)),
            pl.BlockSpec((1, GLA_HEAD_V), lambda b, h, t: (0, 0)),
        ],
        out_specs=pl.BlockSpec((GLA_ROWS, GLA_HEAD_V), lambda b, h, t: (b * nt + t, h)),
        out_shape=jax.ShapeDtypeStruct((M_ROWS, D_INNER), BF16),
        scratch_shapes=[pltpu.VMEM((GLA_HEAD_V, GLA_HEAD_K), F32)],
        compiler_params=_cparams(("parallel", "parallel", "arbitrary")),
    )(main, main, main, main, glow, wg_bf, bg, tri, nw)


def kernel(x, c, positions, ada_w, ada_b, norm_g, da_w_in, da_lam_q1, da_lam_k1, da_lam_q2,
           da_lam_k2, da_subln_w, da_w_out, gla_w_in, gla_w_gate_up, gla_b_gate, gla_norm_w,
           gla_w_out, final_g):
    x2 = x.reshape(M_ROWS, D_MODEL)
    c_pad = jnp.pad(c, ((0, ADA_ROWS - BATCH), (0, 0))).astype(BF16)
    mod = _ada(c_pad, ada_w, ada_b.reshape(DEPTH, 1, 3 * D_MODEL))
    mod = mod.reshape(DEPTH * ADA_ROWS * 3, 1, D_MODEL)
    norm_g3 = norm_g.reshape(DEPTH, 1, D_MODEL)

    inv_freq = ROPE_THETA ** (-jnp.arange(0, DA_HEAD_DIM, 2, dtype=F32) / DA_HEAD_DIM)
    cos, sin = _rope_tables(positions.reshape(BATCH, 1, SEQ), inv_freq.reshape(ROPE_HALF, 1))

    lambda_init = 0.8 - 0.6 * math.exp(-0.3 * 0)
    qkvz = _inproj_da(x2, norm_g3, mod, da_w_in[0].astype(BF16), cos, sin, layer=0)
    lamv = jnp.concatenate([da_lam_q1, da_lam_k1, da_lam_q2, da_lam_k2], axis=0)
    lamv = jnp.pad(lamv, ((0, 4), (0, LANES - DA_HEAD_DIM)))
    o = _diff_attn(qkvz, lamv, da_subln_w[0].reshape(DA_V_DIM, 1), lambda_init)
    x2 = _outproj(o, da_w_out[0].astype(BF16), x2, mod, final_g.reshape(1, D_MODEL),
                  layer=0, final_norm=False)

    w = gla_w_in[0]
    wg_in = jnp.pad(w[:, GLA_MAIN:], ((0, 0), (0, LANES - GLA_GATE_RANK))).astype(BF16)
    main, glow = _inproj_gla(x2, norm_g3, mod, w[:, :GLA_MAIN].astype(BF16), wg_in, layer=1)
    wg_up = jnp.pad(gla_w_gate_up[0], ((0, LANES - GLA_GATE_RANK), (0, 0))).astype(BF16)
    r = jnp.arange(GLA_ROWS)
    tri = ((r[:, None] >= r[None, :])
           & (r[:, None] // GLA_CHUNK == r[None, :] // GLA_CHUNK)).astype(BF16)
    o = _gla(main, glow, wg_up, gla_b_gate, tri, gla_norm_w)
    out = _outproj(o, gla_w_out[0].astype(BF16), x2, mod, final_g.reshape(1, D_MODEL),
                   layer=1, final_norm=True)
    return out.reshape(BATCH, SEQ, D_MODEL)
```

```python
import functools
import math

import jax
import jax.numpy as jnp
from jax import lax
from jax.experimental import pallas as pl
from jax.experimental.pallas import tpu as pltpu

F32 = jnp.float32
BF16 = jnp.bfloat16

D_MODEL = 1024
BATCH = 4
SEQ = 4096
DEPTH = 2
D_INNER = 2 * D_MODEL
EPS = 1e-6
M_ROWS = BATCH * SEQ

DA_HEADS = 16
DA_HEAD_DIM = 64
DA_V_DIM = 128
ROPE_THETA = 10000.0
ROPE_HALF = DA_HEAD_DIM // 2

GLA_HEADS = 4
GLA_DK = 1024
GLA_HEAD_K = 256
GLA_HEAD_V = 512
GLA_GATE_RANK = 16
GLA_GATE_TAU = 16.0
GLA_CHUNK = 64
GLA_MAIN = 2 * GLA_DK + 2 * D_INNER

LANES = 128
ADA_ROWS = 16
VMEM_LIMIT = 56 * 1024 * 1024

TM_IN, TN_IN = 1024, 1024
RC_IN = 256
RC_OUT = 128
TM_OUT = 512
TQ = 1024
TK = 256
NSUB = TQ // TK
CB = 256
assert TQ % TK == 0 and TQ % CB == 0
ONES_ROWS = 16
GLA_ROWS = 256
TM_ROPE = 1024
TN_ADA = 1536

NEG = -1e30


def _cparams(sem):
    return pltpu.CompilerParams(dimension_semantics=sem, vmem_limit_bytes=VMEM_LIMIT)


def _ada_kernel(c_ref, w_ref, b_ref, o_ref):
    o_ref[...] = jnp.dot(c_ref[...], w_ref[...].astype(BF16),
                         preferred_element_type=F32) + b_ref[...]


def _ada(c_pad, ada_w, ada_b3):
    n = 3 * D_MODEL
    return pl.pallas_call(
        _ada_kernel,
        grid=(DEPTH, n // TN_ADA),
        in_specs=[
            pl.BlockSpec((ADA_ROWS, D_MODEL), lambda l, j: (0, 0)),
            pl.BlockSpec((None, D_MODEL, TN_ADA), lambda l, j: (l, 0, j)),
            pl.BlockSpec((None, 1, TN_ADA), lambda l, j: (l, 0, j)),
        ],
        out_specs=pl.BlockSpec((None, ADA_ROWS, TN_ADA), lambda l, j: (l, 0, j)),
        out_shape=jax.ShapeDtypeStruct((DEPTH, ADA_ROWS, n), F32),
        compiler_params=_cparams(("parallel", "parallel")),
    )(c_pad, ada_w, ada_b3)


def _rope_kernel(pos_ref, invf_ref, cos_ref, sin_ref):
    ang = invf_ref[...] * pos_ref[...].astype(F32)
    c = jnp.cos(ang)
    s = jnp.sin(ang)
    cos_ref[...] = jnp.concatenate([c, c, c, c], axis=0).T
    sin_ref[...] = jnp.concatenate([-s, -s, s, s], axis=0).T


def _rope_tables(pos3, invf):
    nt = SEQ // TM_ROPE
    return pl.pallas_call(
        _rope_kernel,
        grid=(BATCH, nt),
        in_specs=[
            pl.BlockSpec((None, 1, TM_ROPE), lambda b, t: (b, 0, t)),
            pl.BlockSpec((ROPE_HALF, 1), lambda b, t: (0, 0)),
        ],
        out_specs=[
            pl.BlockSpec((TM_ROPE, LANES), lambda b, t: (b * nt + t, 0)),
            pl.BlockSpec((TM_ROPE, LANES), lambda b, t: (b * nt + t, 0)),
        ],
        out_shape=[jax.ShapeDtypeStruct((M_ROWS, LANES), F32)] * 2,
        compiler_params=_cparams(("parallel", "parallel")),
    )(pos3, invf)


def _pipeline_row_chunks(n_chunks, chunk_dot, chunk_store):
    acc = chunk_dot(0)
    for r in range(n_chunks):
        nxt = chunk_dot(r + 1) if r + 1 < n_chunks else None
        chunk_store(r, acc)
        acc = nxt


def _norm_modulate(x, g, scale, shift):
    ms = jnp.mean(x * x, axis=-1, keepdims=True)
    return x * lax.rsqrt(ms + EPS) * (g * (1.0 + scale)) + shift


def _inproj_da_kernel(x_ref, g_ref, scale_ref, shift_ref, w_ref, cos_ref, sin_ref,
                      o_ref, h_scr, *, n_q_tiles, n_rope_tiles, q_scale):
    j = pl.program_id(1)

    @pl.when(j == 0)
    def _():
        h_scr[...] = _norm_modulate(x_ref[...], g_ref[...], scale_ref[...],
                                    shift_ref[...]).astype(BF16)

    is_rope = j < n_rope_tiles
    sc = jnp.where(j < n_q_tiles, q_scale, 1.0).astype(F32)
    cos = jnp.where(is_rope, cos_ref[...] * sc, 1.0)
    sin = jnp.where(is_rope, sin_ref[...] * sc, 0.0)
    n_chunks = TM_IN // RC_IN

    def chunk_dot(r):
        return jnp.dot(h_scr[r * RC_IN:(r + 1) * RC_IN, :], w_ref[...],
                       preferred_element_type=F32)

    def chunk_store(r, acc):
        rows = slice(r * RC_IN, (r + 1) * RC_IN)
        for gidx in range(acc.shape[1] // LANES):
            cols = slice(gidx * LANES, (gidx + 1) * LANES)
            t = acc[:, cols]
            o_ref[rows, cols] = (t * cos[rows, :]
                                 + pltpu.roll(t, LANES // 2, 1) * sin[rows, :]).astype(BF16)

    _pipeline_row_chunks(n_chunks, chunk_dot, chunk_store)


def _inproj_da(x2, norm_g, mod, w_bf, cos, sin, layer):
    n = w_bf.shape[1]
    seg_tiles = D_INNER // TN_IN
    tiles_per_batch = SEQ // TM_IN

    def mod_idx(which):
        return lambda i, j: ((layer * ADA_ROWS + i // tiles_per_batch) * 3 + which, 0, 0)

    kern = functools.partial(_inproj_da_kernel, n_q_tiles=seg_tiles,
                             n_rope_tiles=2 * seg_tiles,
                             q_scale=DA_HEAD_DIM ** -0.5 * math.log2(math.e))
    return pl.pallas_call(
        kern,
        grid=(M_ROWS // TM_IN, n // TN_IN),
        in_specs=[
            pl.BlockSpec((TM_IN, D_MODEL), lambda i, j: (i, 0)),
            pl.BlockSpec((None, 1, D_MODEL), lambda i, j: (layer, 0, 0)),
            pl.BlockSpec((None, 1, D_MODEL), mod_idx(1)),
            pl.BlockSpec((None, 1, D_MODEL), mod_idx(0)),
            pl.BlockSpec((D_MODEL, TN_IN), lambda i, j: (0, j)),
            pl.BlockSpec((TM_IN, LANES), lambda i, j: (i, 0)),
            pl.BlockSpec((TM_IN, LANES), lambda i, j: (i, 0)),
        ],
        out_specs=pl.BlockSpec((None, TM_IN, TN_IN),
                               lambda i, j: (j // seg_tiles, i, j % seg_tiles)),
        out_shape=jax.ShapeDtypeStruct((4, M_ROWS, D_INNER), BF16),
        scratch_shapes=[pltpu.VMEM((TM_IN, D_MODEL), BF16)],
        compiler_params=_cparams(("parallel", "arbitrary")),
    )(x2, norm_g, mod, mod, w_bf, cos, sin)


def _inproj_gla_kernel(x_ref, g_ref, scale_ref, shift_ref, w_ref, wg_ref,
                       o_ref, glow_ref, h_scr):
    j = pl.program_id(1)

    @pl.when(j == 0)
    def _():
        h = _norm_modulate(x_ref[...], g_ref[...], scale_ref[...], shift_ref[...]).astype(BF16)
        h_scr[...] = h
        glow_ref[...] = jnp.dot(h, wg_ref[...], preferred_element_type=F32)

    def chunk_dot(r):
        return jnp.dot(h_scr[r * RC_IN:(r + 1) * RC_IN, :], w_ref[...],
                       preferred_element_type=F32)

    def chunk_store(r, acc):
        o_ref[r * RC_IN:(r + 1) * RC_IN, :] = acc.astype(BF16)

    _pipeline_row_chunks(TM_IN // RC_IN, chunk_dot, chunk_store)


def _inproj_gla(x2, norm_g, mod, w_bf, wg_bf, layer):
    n = w_bf.shape[1]
    tiles_per_batch = SEQ // TM_IN

    def mod_idx(which):
        return lambda i, j: ((layer * ADA_ROWS + i // tiles_per_batch) * 3 + which, 0, 0)

    return pl.pallas_call(
        _inproj_gla_kernel,
        grid=(M_ROWS // TM_IN, n // TN_IN),
        in_specs=[
            pl.BlockSpec((TM_IN, D_MODEL), lambda i, j: (i, 0)),
            pl.BlockSpec((None, 1, D_MODEL), lambda i, j: (layer, 0, 0)),
            pl.BlockSpec((None, 1, D_MODEL), mod_idx(1)),
            pl.BlockSpec((None, 1, D_MODEL), mod_idx(0)),
            pl.BlockSpec((D_MODEL, TN_IN), lambda i, j: (0, j)),
            pl.BlockSpec((D_MODEL, LANES), lambda i, j: (0, 0)),
        ],
        out_specs=[
            pl.BlockSpec((TM_IN, TN_IN), lambda i, j: (i, j)),
            pl.BlockSpec((TM_IN, LANES), lambda i, j: (i, 0)),
        ],
        out_shape=[jax.ShapeDtypeStruct((M_ROWS, n), BF16),
                   jax.ShapeDtypeStruct((M_ROWS, LANES), F32)],
        scratch_shapes=[pltpu.VMEM((TM_IN, D_MODEL), BF16)],
        compiler_params=_cparams(("parallel", "arbitrary")),
    )(x2, norm_g, mod, mod, w_bf, wg_bf)


def _outproj_kernel(o_ref, w_ref, x_ref, gate_ref, fg_ref, out_ref, *, final_norm):
    def chunk_dot(r):
        return jnp.dot(o_ref[r * RC_OUT:(r + 1) * RC_OUT, :], w_ref[...],
                       preferred_element_type=F32)

    def chunk_store(r, y):
        rows = slice(r * RC_OUT, (r + 1) * RC_OUT)
        x = x_ref[rows, :] + gate_ref[...] * y
        if final_norm:
            ms = jnp.mean(x * x, axis=-1, keepdims=True)
            x = x * lax.rsqrt(ms + EPS) * fg_ref[...]
        out_ref[rows, :] = x

    _pipeline_row_chunks(TM_OUT // RC_OUT, chunk_dot, chunk_store)


def _outproj(o, w_bf, x2, mod, final_g, layer, final_norm):
    tiles_per_batch = SEQ // TM_OUT
    kern = functools.partial(_outproj_kernel, final_norm=final_norm)
    return pl.pallas_call(
        kern,
        grid=(M_ROWS // TM_OUT,),
        in_specs=[
            pl.BlockSpec((TM_OUT, D_INNER), lambda i: (i, 0)),
            pl.BlockSpec((D_INNER, D_MODEL), lambda i: (0, 0)),
            pl.BlockSpec((TM_OUT, D_MODEL), lambda i: (i, 0)),
            pl.BlockSpec((None, 1, D_MODEL),
                         lambda i: ((layer * ADA_ROWS + i // tiles_per_batch) * 3 + 2, 0, 0)),
            pl.BlockSpec((1, D_MODEL), lambda i: (0, 0)),
        ],
        out_specs=pl.BlockSpec((TM_OUT, D_MODEL), lambda i: (i, 0)),
        out_shape=jax.ShapeDtypeStruct((M_ROWS, D_MODEL), F32),
        compiler_params=_cparams(("parallel",)),
    )(o, w_bf, x2, mod, final_g)


def _diff_attn_kernel(q_ref, k_ref, v_ref, z_ref, lamv_ref, subw_ref, o_ref,
                      vt_scr, m_scr, acc_scr, p_scr, alpha_scr, *, lambda_init):
    qi = pl.program_id(2)

    @pl.when(qi == 0)
    def _():
        vt_scr[0:DA_V_DIM, :] = v_ref[...].astype(F32).T.astype(BF16)
        vt_scr[DA_V_DIM:, :] = jnp.ones((ONES_ROWS, SEQ), BF16)

    q = q_ref[...]
    lane = lax.broadcasted_iota(jnp.int32, q.shape, 1)
    zero = jnp.zeros_like(q)
    is_a = (lane % DA_HEAD_DIM) < ROPE_HALF
    qs = jnp.concatenate([jnp.where(is_a, q, zero), jnp.where(is_a, zero, q)],
                         axis=0)

    m_scr[...] = jnp.full(m_scr.shape, NEG, F32)
    acc_scr[...] = jnp.zeros(acc_scr.shape, F32)

    def deferred_pv(prev_start, cs):
        vt = vt_scr[:, pl.ds(pl.multiple_of(prev_start, TK), TK)]
        acc_scr[:, cs] = alpha_scr[:, cs] * acc_scr[:, cs] + jnp.dot(
            vt, p_scr[:, cs], preferred_element_type=F32)

    def block_masked_out(diag_offset, c):
        return diag_offset is not None and diag_offset > (c * CB) % TQ + CB - 1

    def key_round(start, diag_offset, prev_start, prev_diag_offset=None):
        k = k_ref[pl.ds(pl.multiple_of(start, TK), TK), :]
        for c in range(2 * TQ // CB):
            cs = slice(c * CB, (c + 1) * CB)
            q_lo = (c * CB) % TQ
            masked_out = block_masked_out(diag_offset, c)
            needs_mask = (diag_offset is not None and not masked_out
                          and diag_offset + TK - 1 > q_lo)
            if not masked_out:
                st = lax.dot_general(k, qs[cs], (((1,), (1,)), ((), ())),
                                     preferred_element_type=F32)
            if prev_start is not None and not block_masked_out(prev_diag_offset, c):
                deferred_pv(prev_start, cs)
            if masked_out:
                alpha_scr[:, cs] = jnp.ones((1, CB), F32)
                p_scr[:, cs] = jnp.zeros((TK, CB), BF16)
                continue
            if needs_mask:
                kpos = lax.broadcasted_iota(jnp.int32, st.shape, 0) + diag_offset
                qpos = lax.broadcasted_iota(jnp.int32, st.shape, 1) + q_lo
                st = jnp.where(kpos <= qpos, st, NEG)
            m_prev = m_scr[:, cs]
            m_new = jnp.maximum(m_prev, jnp.max(st, axis=0, keepdims=True))
            alpha_scr[:, cs] = jnp.exp2(m_prev - m_new)
            p_scr[:, cs] = jnp.exp2(st - m_new).astype(BF16)
            m_scr[:, cs] = m_new

    diag = qi * TQ
    last_sub = (NSUB - 1) * TK
    key_round(diag, 0, None)
    for u in range(1, NSUB):
        key_round(diag + u * TK, u * TK, diag + (u - 1) * TK, (u - 1) * TK)

    def body(kc, carry):
        base = kc * TQ
        key_round(base, None, jnp.where(kc == 0, diag + last_sub, base - TK))
        for u in range(1, NSUB):
            key_round(base + u * TK, None, base + (u - 1) * TK)
        return carry

    lax.fori_loop(0, qi, body, 0)
    last_start = jnp.where(qi == 0, last_sub, diag - TK)
    for c in range(2 * TQ // CB):
        deferred_pv(last_start, slice(c * CB, (c + 1) * CB))

    lamv = lamv_ref[...]
    lam = (jnp.exp(jnp.sum(lamv[0:1, :] * lamv[1:2, :], keepdims=True))
           - jnp.exp(jnp.sum(lamv[2:3, :] * lamv[3:4, :], keepdims=True)) + lambda_init)
    acc = acc_scr[...]
    on = acc[0:DA_V_DIM, :] / acc[DA_V_DIM:DA_V_DIM + 1, :]
    ot = on[:, :TQ] - lam * on[:, TQ:]
    ms = jnp.mean(ot * ot, axis=0, keepdims=True)
    ot = ot * lax.rsqrt(ms + EPS) * (subw_ref[...] * (1.0 - lambda_init))
    z = z_ref[...].astype(F32)
    o_ref[...] = (ot.T * (z * jax.nn.sigmoid(z))).astype(BF16)


def _diff_attn(qkvz, lamv, subw_col, lambda_init):
    nq = SEQ // TQ
    kern = functools.partial(_diff_attn_kernel, lambda_init=lambda_init)
    return pl.pallas_call(
        kern,
        grid=(BATCH, DA_HEADS, nq),
        in_specs=[
            pl.BlockSpec((None, TQ, LANES), lambda b, h, i: (0, b * nq + i, h)),
            pl.BlockSpec((None, SEQ, LANES), lambda b, h, i: (1, b, h)),
            pl.BlockSpec((None, SEQ, LANES), lambda b, h, i: (2, b, h)),
            pl.BlockSpec((None, TQ, LANES), lambda b, h, i: (3, b * nq + i, h)),
            pl.BlockSpec((8, LANES), lambda b, h, i: (0, 0)),
            pl.BlockSpec((DA_V_DIM, 1), lambda b, h, i: (0, 0)),
        ],
        out_specs=pl.BlockSpec((TQ, LANES), lambda b, h, i: (b * nq + i, h)),
        out_shape=jax.ShapeDtypeStruct((M_ROWS, D_INNER), BF16),
        scratch_shapes=[
            pltpu.VMEM((DA_V_DIM + ONES_ROWS, SEQ), BF16),
            pltpu.VMEM((1, 2 * TQ), F32),
            pltpu.VMEM((DA_V_DIM + ONES_ROWS, 2 * TQ), F32),
            pltpu.VMEM((TK, 2 * TQ), BF16),
            pltpu.VMEM((1, 2 * TQ), F32),
        ],
        compiler_params=_cparams(("parallel", "parallel", "arbitrary")),
    )(qkvz, qkvz, qkvz, qkvz, lamv, subw_col)


def _gla_kernel(q_ref, k_ref, v_ref, z_ref, g_ref, wg_ref, bg_ref, tri_ref, nw_ref,
                o_ref, st_scr):
    t = pl.program_id(1)

    @pl.when(t == 0)
    def _():
        st_scr[...] = jnp.zeros(st_scr.shape, F32)

    nc = GLA_ROWS // GLA_CHUNK
    pre = jnp.dot(g_ref[...].astype(BF16), wg_ref[...], preferred_element_type=F32) + bg_ref[...]
    log_a = (jnp.minimum(pre, 0.0) - jnp.log(1.0 + jnp.exp(-jnp.abs(pre)))) * (1.0 / GLA_GATE_TAU)
    hi = log_a.astype(BF16)
    lo = (log_a - hi.astype(F32)).astype(BF16)
    tri = tri_ref[...]
    bcum = (jnp.dot(tri, hi, preferred_element_type=F32)
            + jnp.dot(tri, lo, preferred_element_type=F32))

    row = lax.broadcasted_iota(jnp.int32, (GLA_ROWS, GLA_ROWS), 0)
    col = lax.broadcasted_iota(jnp.int32, (GLA_ROWS, GLA_ROWS), 1)
    causal = col <= row
    nw = nw_ref[...]
    nt_dims = (((1,), (1,)), ((), ()))

    for h in range(GLA_HEADS):
        ks = slice(h * GLA_HEAD_K, (h + 1) * GLA_HEAD_K)
        vs = slice(h * GLA_HEAD_V, (h + 1) * GLA_HEAD_V)
        bh = bcum[:, ks]
        b_end = [bh[(i + 1) * GLA_CHUNK - 1:(i + 1) * GLA_CHUNK, :] for i in range(nc)]
        q_dec, q_int, k_inv, k_end, k_st = [], [], [], [], []
        for i in range(nc):
            rows = slice(i * GLA_CHUNK, (i + 1) * GLA_CHUNK)
            b = bh[rows, :] if i == 0 else bh[rows, :] - b_end[i - 1]
            qd = q_ref[rows, ks].astype(F32) * (GLA_HEAD_K ** -0.5) * jnp.exp(b)
            ki = k_ref[rows, ks].astype(F32) * jnp.exp(-b)
            ke = ki * jnp.exp(b_end[i] if i == 0 else b_end[i] - b_end[i - 1])
            q_dec.append(qd.astype(BF16))
            q_int.append((qd if i == 0 else qd * jnp.exp(b_end[i - 1])).astype(BF16))
            k_inv.append(ki.astype(BF16))
            k_end.append(ke)
            k_st.append((ke if i == nc - 1 else ke * jnp.exp(b_end[nc - 1] - b_end[i]))
                        .astype(BF16))
        attn_rows = []
        for i in range(nc):
            pieces = []
            for j in range(i):
                kj = k_end[j] if j == i - 1 else k_end[j] * jnp.exp(b_end[i - 1] - b_end[j])
                pieces.append(kj.astype(BF16))
            pieces.append(k_inv[i])
            if i < nc - 1:
                pieces.append(jnp.zeros(((nc - 1 - i) * GLA_CHUNK, GLA_HEAD_K), BF16))
            attn_rows.append(lax.dot_general(q_dec[i], jnp.concatenate(pieces, axis=0), nt_dims,
                                             preferred_element_type=F32))
        attn = jnp.where(causal, jnp.concatenate(attn_rows, axis=0), 0.0).astype(BF16)
        st = st_scr[h]
        v = v_ref[:, vs]
        o = jnp.dot(jnp.concatenate([attn] + [jnp.concatenate(q_int, axis=0)], axis=1),
                    jnp.concatenate([v, st.astype(BF16)], axis=0),
                    preferred_element_type=F32)
        dec_col = jnp.broadcast_to(jnp.exp(b_end[nc - 1]), (LANES, GLA_HEAD_K)).T
        dec = jnp.concatenate([dec_col] * (GLA_HEAD_V // LANES), axis=1)
        st_scr[h] = st * dec + lax.dot_general(
            jnp.concatenate(k_st, axis=0), v, (((0,), (0,)), ((), ())),
            preferred_element_type=F32)
        ms = jnp.mean(o * o, axis=-1, keepdims=True)
        z = z_ref[:, vs].astype(F32)
        o_ref[:, vs] = (o * lax.rsqrt(ms + EPS) * nw * (z * jax.nn.sigmoid(z))).astype(BF16)


def _gla(main, glow, wg_bf, bg, tri, nw):
    nt = SEQ // GLA_ROWS
    assert 2 * GLA_DK == D_INNER
    return pl.pallas_call(
        _gla_kernel,
        grid=(BATCH, nt),
        in_specs=[
            pl.BlockSpec((GLA_ROWS, GLA_DK), lambda b, t: (b * nt + t, 0)),
            pl.BlockSpec((GLA_ROWS, GLA_DK), lambda b, t: (b * nt + t, 1)),
            pl.BlockSpec((GLA_ROWS, D_INNER), lambda b, t: (b * nt + t, 1)),
            pl.BlockSpec((GLA_ROWS, D_INNER), lambda b, t: (b * nt + t, 2)),
            pl.BlockSpec((GLA_ROWS, LANES), lambda b, t: (b * nt + t, 0)),
            pl.BlockSpec((LANES, GLA_DK), lambda b, t: (0, 0)),
            pl.BlockSpec((1, GLA_DK), lambda b, t: (0, 0)),
            pl.BlockSpec((GLA_ROWS, GLA_ROWS), lambda b, t: (0, 0)),
            pl.BlockSpec((1, GLA_HEAD_V), lambda b, t: (0, 0)),
        ],
        out_specs=pl.BlockSpec((GLA_ROWS, D_INNER), lambda b, t: (b * nt + t, 0)),
        out_shape=jax.ShapeDtypeStruct((M_ROWS, D_INNER), BF16),
        scratch_shapes=[pltpu.VMEM((GLA_HEADS, GLA_HEAD_K, GLA_HEAD_V), F32)],
        compiler_params=_cparams(("parallel", "arbitrary")),
    )(main, main, main, main, glow, wg_bf, bg, tri, nw)


def _permute_qk_columns(w_in):
    d = w_in.shape[0]
    qk = w_in[:, :2 * D_INNER].reshape(d, 2, DA_HEADS, 2, 2, ROPE_HALF)
    qk = qk.transpose(0, 1, 2, 4, 3, 5).reshape(d, 2 * D_INNER)
    return jnp.concatenate([qk, w_in[:, 2 * D_INNER:]], axis=1)


def kernel(x, c, positions, ada_w, ada_b, norm_g, da_w_in, da_lam_q1, da_lam_k1, da_lam_q2,
           da_lam_k2, da_subln_w, da_w_out, gla_w_in, gla_w_gate_up, gla_b_gate, gla_norm_w,
           gla_w_out, final_g):
    x2 = x.reshape(M_ROWS, D_MODEL)
    c_pad = jnp.pad(c, ((0, ADA_ROWS - BATCH), (0, 0))).astype(BF16)
    mod = _ada(c_pad, ada_w, ada_b.reshape(DEPTH, 1, 3 * D_MODEL))
    mod = mod.reshape(DEPTH * ADA_ROWS * 3, 1, D_MODEL)
    norm_g3 = norm_g.reshape(DEPTH, 1, D_MODEL)

    inv_freq = ROPE_THETA ** (-jnp.arange(0, DA_HEAD_DIM, 2, dtype=F32) / DA_HEAD_DIM)
    cos, sin = _rope_tables(positions.reshape(BATCH, 1, SEQ), inv_freq.reshape(ROPE_HALF, 1))

    lambda_init = 0.8 - 0.6 * math.exp(-0.3 * 0)
    qkvz = _inproj_da(x2, norm_g3, mod, _permute_qk_columns(da_w_in[0]).astype(BF16), cos, sin,
                      layer=0)
    lamv = jnp.concatenate([da_lam_q1, da_lam_k1, da_lam_q2, da_lam_k2], axis=0)
    lamv = jnp.pad(lamv, ((0, 4), (0, LANES - DA_HEAD_DIM)))
    o = _diff_attn(qkvz, lamv, da_subln_w[0].reshape(DA_V_DIM, 1), lambda_init)
    x2 = _outproj(o, da_w_out[0].astype(BF16), x2, mod, final_g.reshape(1, D_MODEL),
                  layer=0, final_norm=False)

    w = gla_w_in[0]
    wg_in = jnp.pad(w[:, GLA_MAIN:], ((0, 0), (0, LANES - GLA_GATE_RANK))).astype(BF16)
    main, glow = _inproj_gla(x2, norm_g3, mod, w[:, :GLA_MAIN].astype(BF16), wg_in, layer=1)
    wg_up = jnp.pad(gla_w_gate_up[0], ((0, LANES - GLA_GATE_RANK), (0, 0))).astype(BF16)
    r = jnp.arange(GLA_ROWS)
    tri = (r[:, None] >= r[None, :]).astype(BF16)
    o = _gla(main, glow, wg_up, gla_b_gate, tri, gla_norm_w)
    out = _outproj(o, gla_w_out[0].astype(BF16), x2, mod, final_g.reshape(1, D_MODEL),
                   layer=1, final_norm=True)
    return out.reshape(BATCH, SEQ, D_MODEL)
```

```python
import functools
import math

import jax
import jax.numpy as jnp
from jax import lax
from jax.experimental import pallas as pl
from jax.experimental.pallas import tpu as pltpu

F32 = jnp.float32
BF16 = jnp.bfloat16

D_MODEL = 1024
BATCH = 4
SEQ = 4096
DEPTH = 2
D_INNER = 2 * D_MODEL
EPS = 1e-6
M_ROWS = BATCH * SEQ

DA_HEADS = 16
DA_HEAD_DIM = 64
DA_V_DIM = 128
ROPE_THETA = 10000.0
ROPE_HALF = DA_HEAD_DIM // 2

GLA_HEADS = 4
GLA_DK = 1024
GLA_HEAD_K = 256
GLA_HEAD_V = 512
GLA_GATE_RANK = 16
GLA_GATE_TAU = 16.0
GLA_CHUNK = 64
GLA_MAIN = 2 * GLA_DK + 2 * D_INNER

LANES = 128
ADA_ROWS = 16
VMEM_LIMIT = 56 * 1024 * 1024

TM_IN, TN_IN = 1024, 1024
RC_IN = 256
RC_OUT = 128
TM_OUT = 512
TQ = 1024
TK = 256
NSUB = TQ // TK
CB = 256
assert TQ % TK == 0 and TQ % CB == 0
ONES_ROWS = 16
GLA_ROWS = 256
TM_ROPE = 1024
TN_ADA = 1536

NEG = -1e30


def _cparams(sem):
    return pltpu.CompilerParams(dimension_semantics=sem, vmem_limit_bytes=VMEM_LIMIT)


def _ada_kernel(c_ref, w_ref, b_ref, o_ref):
    o_ref[...] = jnp.dot(c_ref[...], w_ref[...].astype(BF16),
                         preferred_element_type=F32) + b_ref[...]


def _ada(c_pad, ada_w, ada_b3):
    n = 3 * D_MODEL
    return pl.pallas_call(
        _ada_kernel,
        grid=(DEPTH, n // TN_ADA),
        in_specs=[
            pl.BlockSpec((ADA_ROWS, D_MODEL), lambda l, j: (0, 0)),
            pl.BlockSpec((None, D_MODEL, TN_ADA), lambda l, j: (l, 0, j)),
            pl.BlockSpec((None, 1, TN_ADA), lambda l, j: (l, 0, j)),
        ],
        out_specs=pl.BlockSpec((None, ADA_ROWS, TN_ADA), lambda l, j: (l, 0, j)),
        out_shape=jax.ShapeDtypeStruct((DEPTH, ADA_ROWS, n), F32),
        compiler_params=_cparams(("parallel", "parallel")),
    )(c_pad, ada_w, ada_b3)


def _rope_kernel(pos_ref, invf_ref, cos_ref, sin_ref):
    ang = invf_ref[...] * pos_ref[...].astype(F32)
    c = jnp.cos(ang)
    s = jnp.sin(ang)
    cos_ref[...] = jnp.concatenate([c, c, c, c], axis=0).T
    sin_ref[...] = jnp.concatenate([-s, -s, s, s], axis=0).T


def _rope_tables(pos3, invf):
    nt = SEQ // TM_ROPE
    return pl.pallas_call(
        _rope_kernel,
        grid=(BATCH, nt),
        in_specs=[
            pl.BlockSpec((None, 1, TM_ROPE), lambda b, t: (b, 0, t)),
            pl.BlockSpec((ROPE_HALF, 1), lambda b, t: (0, 0)),
        ],
        out_specs=[
            pl.BlockSpec((TM_ROPE, LANES), lambda b, t: (b * nt + t, 0)),
            pl.BlockSpec((TM_ROPE, LANES), lambda b, t: (b * nt + t, 0)),
        ],
        out_shape=[jax.ShapeDtypeStruct((M_ROWS, LANES), F32)] * 2,
        compiler_params=_cparams(("parallel", "parallel")),
    )(pos3, invf)


def _pipeline_row_chunks(n_chunks, chunk_dot, chunk_store):
    acc = chunk_dot(0)
    for r in range(n_chunks):
        nxt = chunk_dot(r + 1) if r + 1 < n_chunks else None
        chunk_store(r, acc)
        acc = nxt


def _norm_modulate(x, g, scale, shift):
    ms = jnp.mean(x * x, axis=-1, keepdims=True)
    return x * lax.rsqrt(ms + EPS) * (g * (1.0 + scale)) + shift


def _inproj_da_kernel(x_ref, g_ref, scale_ref, shift_ref, w_ref, cos_ref, sin_ref,
                      o_ref, h_scr, *, n_q_tiles, n_rope_tiles, q_scale):
    j = pl.program_id(1)

    @pl.when(j == 0)
    def _():
        h_scr[...] = _norm_modulate(x_ref[...], g_ref[...], scale_ref[...],
                                    shift_ref[...]).astype(BF16)

    is_rope = j < n_rope_tiles
    sc = jnp.where(j < n_q_tiles, q_scale, 1.0).astype(F32)
    cos = jnp.where(is_rope, cos_ref[...] * sc, 1.0)
    sin = jnp.where(is_rope, sin_ref[...] * sc, 0.0)
    n_chunks = TM_IN // RC_IN

    def chunk_dot(r):
        return jnp.dot(h_scr[r * RC_IN:(r + 1) * RC_IN, :], w_ref[...],
                       preferred_element_type=F32)

    def chunk_store(r, acc):
        rows = slice(r * RC_IN, (r + 1) * RC_IN)
        for gidx in range(acc.shape[1] // LANES):
            cols = slice(gidx * LANES, (gidx + 1) * LANES)
            t = acc[:, cols]
            o_ref[rows, cols] = (t * cos[rows, :]
                                 + pltpu.roll(t, LANES // 2, 1) * sin[rows, :]).astype(BF16)

    _pipeline_row_chunks(n_chunks, chunk_dot, chunk_store)


def _inproj_da(x2, norm_g, mod, w_bf, cos, sin, layer):
    n = w_bf.shape[1]
    seg_tiles = D_INNER // TN_IN
    tiles_per_batch = SEQ // TM_IN

    def mod_idx(which):
        return lambda i, j: ((layer * ADA_ROWS + i // tiles_per_batch) * 3 + which, 0, 0)

    kern = functools.partial(_inproj_da_kernel, n_q_tiles=seg_tiles,
                             n_rope_tiles=2 * seg_tiles,
                             q_scale=DA_HEAD_DIM ** -0.5 * math.log2(math.e))
    return pl.pallas_call(
        kern,
        grid=(M_ROWS // TM_IN, n // TN_IN),
        in_specs=[
            pl.BlockSpec((TM_IN, D_MODEL), lambda i, j: (i, 0)),
            pl.BlockSpec((None, 1, D_MODEL), lambda i, j: (layer, 0, 0)),
            pl.BlockSpec((None, 1, D_MODEL), mod_idx(1)),
            pl.BlockSpec((None, 1, D_MODEL), mod_idx(0)),
            pl.BlockSpec((D_MODEL, TN_IN), lambda i, j: (0, j)),
            pl.BlockSpec((TM_IN, LANES), lambda i, j: (i, 0)),
            pl.BlockSpec((TM_IN, LANES), lambda i, j: (i, 0)),
        ],
        out_specs=pl.BlockSpec((None, TM_IN, TN_IN),
                               lambda i, j: (j // seg_tiles, i, j % seg_tiles)),
        out_shape=jax.ShapeDtypeStruct((4, M_ROWS, D_INNER), BF16),
        scratch_shapes=[pltpu.VMEM((TM_IN, D_MODEL), BF16)],
        compiler_params=_cparams(("parallel", "arbitrary")),
    )(x2, norm_g, mod, mod, w_bf, cos, sin)


def _inproj_gla_kernel(x_ref, g_ref, scale_ref, shift_ref, w_ref, wg_ref,
                       o_ref, glow_ref, h_scr):
    j = pl.program_id(1)

    @pl.when(j == 0)
    def _():
        h = _norm_modulate(x_ref[...], g_ref[...], scale_ref[...], shift_ref[...]).astype(BF16)
        h_scr[...] = h
        glow_ref[...] = jnp.dot(h, wg_ref[...], preferred_element_type=F32)

    def chunk_dot(r):
        return jnp.dot(h_scr[r * RC_IN:(r + 1) * RC_IN, :], w_ref[...],
                       preferred_element_type=F32)

    def chunk_store(r, acc):
        o_ref[r * RC_IN:(r + 1) * RC_IN, :] = acc.astype(BF16)

    _pipeline_row_chunks(TM_IN // RC_IN, chunk_dot, chunk_store)


def _inproj_gla(x2, norm_g, mod, w_bf, wg_bf, layer):
    n = GLA_MAIN
    tiles_per_batch = SEQ // TM_IN

    def mod_idx(which):
        return lambda i, j: ((layer * ADA_ROWS + i // tiles_per_batch) * 3 + which, 0, 0)

    return pl.pallas_call(
        _inproj_gla_kernel,
        grid=(M_ROWS // TM_IN, n // TN_IN),
        in_specs=[
            pl.BlockSpec((TM_IN, D_MODEL), lambda i, j: (i, 0)),
            pl.BlockSpec((None, 1, D_MODEL), lambda i, j: (layer, 0, 0)),
            pl.BlockSpec((None, 1, D_MODEL), mod_idx(1)),
            pl.BlockSpec((None, 1, D_MODEL), mod_idx(0)),
            pl.BlockSpec((D_MODEL, TN_IN), lambda i, j: (0, j)),
            pl.BlockSpec((D_MODEL, LANES), lambda i, j: (0, 0)),
        ],
        out_specs=[
            pl.BlockSpec((TM_IN, TN_IN), lambda i, j: (i, j)),
            pl.BlockSpec((TM_IN, LANES), lambda i, j: (i, 0)),
        ],
        out_shape=[jax.ShapeDtypeStruct((M_ROWS, n), BF16),
                   jax.ShapeDtypeStruct((M_ROWS, LANES), F32)],
        scratch_shapes=[pltpu.VMEM((TM_IN, D_MODEL), BF16)],
        compiler_params=_cparams(("parallel", "arbitrary")),
    )(x2, norm_g, mod, mod, w_bf, wg_bf)


def _outproj_kernel(o_ref, w_ref, x_ref, gate_ref, fg_ref, out_ref, *, final_norm):
    def chunk_dot(r):
        return jnp.dot(o_ref[r * RC_OUT:(r + 1) * RC_OUT, :], w_ref[...],
                       preferred_element_type=F32)

    def chunk_store(r, y):
        rows = slice(r * RC_OUT, (r + 1) * RC_OUT)
        x = x_ref[rows, :] + gate_ref[...] * y
        if final_norm:
            ms = jnp.mean(x * x, axis=-1, keepdims=True)
            x = x * lax.rsqrt(ms + EPS) * fg_ref[...]
        out_ref[rows, :] = x

    _pipeline_row_chunks(TM_OUT // RC_OUT, chunk_dot, chunk_store)


def _outproj(o, w_bf, x2, mod, final_g, layer, final_norm):
    tiles_per_batch = SEQ // TM_OUT
    kern = functools.partial(_outproj_kernel, final_norm=final_norm)
    return pl.pallas_call(
        kern,
        grid=(M_ROWS // TM_OUT,),
        in_specs=[
            pl.BlockSpec((TM_OUT, D_INNER), lambda i: (i, 0)),
            pl.BlockSpec((D_INNER, D_MODEL), lambda i: (0, 0)),
            pl.BlockSpec((TM_OUT, D_MODEL), lambda i: (i, 0)),
            pl.BlockSpec((None, 1, D_MODEL),
                         lambda i: ((layer * ADA_ROWS + i // tiles_per_batch) * 3 + 2, 0, 0)),
            pl.BlockSpec((1, D_MODEL), lambda i: (0, 0)),
        ],
        out_specs=pl.BlockSpec((TM_OUT, D_MODEL), lambda i: (i, 0)),
        out_shape=jax.ShapeDtypeStruct((M_ROWS, D_MODEL), F32),
        compiler_params=_cparams(("parallel",)),
    )(o, w_bf, x2, mod, final_g)


def _diff_attn_kernel(q_ref, k_ref, v_ref, z_ref, lamv_ref, subw_ref, o_ref,
                      vt_scr, m_scr, acc_scr, p_scr, alpha_scr, qst_scr, *, lambda_init):
    qi = pl.program_id(2)

    @pl.when(qi == 0)
    def _():
        vt_scr[0:DA_V_DIM, :] = v_ref[...].astype(F32).T.astype(BF16)
        vt_scr[DA_V_DIM:, :] = jnp.ones((ONES_ROWS, SEQ), BF16)

    qt = q_ref[...].astype(F32).T
    feat = lax.broadcasted_iota(jnp.int32, qt.shape, 0)
    is_a = (feat % DA_HEAD_DIM) < ROPE_HALF
    qst_scr[:, 0:TQ] = jnp.where(is_a, qt, 0.0).astype(BF16)
    qst_scr[:, TQ:] = jnp.where(is_a, 0.0, qt).astype(BF16)

    m_scr[...] = jnp.full(m_scr.shape, NEG, F32)
    acc_scr[...] = jnp.zeros(acc_scr.shape, F32)

    def deferred_pv(prev_start, cs):
        vt = vt_scr[:, pl.ds(pl.multiple_of(prev_start, TK), TK)]
        acc_scr[:, cs] = alpha_scr[:, cs] * acc_scr[:, cs] + jnp.dot(
            vt, p_scr[:, cs], preferred_element_type=F32)

    def block_masked_out(diag_offset, c):
        return diag_offset is not None and diag_offset > (c * CB) % TQ + CB - 1

    def key_round(start, diag_offset, prev_start, prev_diag_offset=None):
        k = k_ref[pl.ds(pl.multiple_of(start, TK), TK), :]
        for c in range(2 * TQ // CB):
            cs = slice(c * CB, (c + 1) * CB)
            q_lo = (c * CB) % TQ
            masked_out = block_masked_out(diag_offset, c)
            needs_mask = (diag_offset is not None and not masked_out
                          and diag_offset + TK - 1 > q_lo)
            if not masked_out:
                st = jnp.dot(k, qst_scr[:, cs],
                             preferred_element_type=F32)
            if prev_start is not None and not block_masked_out(prev_diag_offset, c):
                deferred_pv(prev_start, cs)
            if masked_out:
                alpha_scr[:, cs] = jnp.ones((1, CB), F32)
                p_scr[:, cs] = jnp.zeros((TK, CB), BF16)
                continue
            if needs_mask:
                kpos = lax.broadcasted_iota(jnp.int32, st.shape, 0) + diag_offset
                qpos = lax.broadcasted_iota(jnp.int32, st.shape, 1) + q_lo
                st = jnp.where(kpos <= qpos, st, NEG)
            m_prev = m_scr[:, cs]
            m_new = jnp.maximum(m_prev, jnp.max(st, axis=0, keepdims=True))
            alpha_scr[:, cs] = jnp.exp2(m_prev - m_new)
            p_scr[:, cs] = jnp.exp2(st - m_new).astype(BF16)
            m_scr[:, cs] = m_new

    diag = qi * TQ
    last_sub = (NSUB - 1) * TK
    key_round(diag, 0, None)
    for u in range(1, NSUB):
        key_round(diag + u * TK, u * TK, diag + (u - 1) * TK, (u - 1) * TK)

    def body(kc, carry):
        base = kc * TQ
        key_round(base, None, jnp.where(kc == 0, diag + last_sub, base - TK))
        for u in range(1, NSUB):
            key_round(base + u * TK, None, base + (u - 1) * TK)
        return carry

    lax.fori_loop(0, qi, body, 0)
    last_start = jnp.where(qi == 0, last_sub, diag - TK)
    lamv = lamv_ref[...]
    lam = (jnp.exp(jnp.sum(lamv[0:1, :] * lamv[1:2, :], keepdims=True))
           - jnp.exp(jnp.sum(lamv[2:3, :] * lamv[3:4, :], keepdims=True)) + lambda_init)
    subw = subw_ref[...] * (1.0 - lambda_init)

    for qb in range(TQ // CB):
        cs1 = slice(qb * CB, (qb + 1) * CB)
        cs2 = slice(TQ + qb * CB, TQ + (qb + 1) * CB)
        deferred_pv(last_start, cs1)
        deferred_pv(last_start, cs2)
        on1 = acc_scr[0:DA_V_DIM, cs1] / acc_scr[DA_V_DIM:DA_V_DIM + 1, cs1]
        on2 = acc_scr[0:DA_V_DIM, cs2] / acc_scr[DA_V_DIM:DA_V_DIM + 1, cs2]
        ot = on1 - lam * on2
        ms = jnp.mean(ot * ot, axis=0, keepdims=True)
        ot = ot * lax.rsqrt(ms + EPS) * subw
        z = z_ref[cs1, :].astype(F32)
        o_ref[cs1, :] = (ot.T * (z * jax.nn.sigmoid(z))).astype(BF16)


def _diff_attn(qkvz, lamv, subw_col, lambda_init):
    nq = SEQ // TQ
    kern = functools.partial(_diff_attn_kernel, lambda_init=lambda_init)
    return pl.pallas_call(
        kern,
        grid=(BATCH, DA_HEADS, nq),
        in_specs=[
            pl.BlockSpec((None, TQ, LANES), lambda b, h, i: (0, b * nq + i, h)),
            pl.BlockSpec((None, SEQ, LANES), lambda b, h, i: (1, b, h)),
            pl.BlockSpec((None, SEQ, LANES), lambda b, h, i: (2, b, h)),
            pl.BlockSpec((None, TQ, LANES), lambda b, h, i: (3, b * nq + i, h)),
            pl.BlockSpec((8, LANES), lambda b, h, i: (0, 0)),
            pl.BlockSpec((DA_V_DIM, 1), lambda b, h, i: (0, 0)),
        ],
        out_specs=pl.BlockSpec((TQ, LANES), lambda b, h, i: (b * nq + i, h)),
        out_shape=jax.ShapeDtypeStruct((M_ROWS, D_INNER), BF16),
        scratch_shapes=[
            pltpu.VMEM((DA_V_DIM + ONES_ROWS, SEQ), BF16),
            pltpu.VMEM((1, 2 * TQ), F32),
            pltpu.VMEM((DA_V_DIM + ONES_ROWS, 2 * TQ), F32),
            pltpu.VMEM((TK, 2 * TQ), BF16),
            pltpu.VMEM((1, 2 * TQ), F32),
            pltpu.VMEM((LANES, 2 * TQ), BF16),
        ],
        compiler_params=_cparams(("parallel", "parallel", "arbitrary")),
    )(qkvz, qkvz, qkvz, qkvz, lamv, subw_col)


def _gla_kernel(q_ref, k_ref, v_ref, z_ref, g_ref, wg_ref, bg_ref, tri_ref, nw_ref,
                o_ref, st_scr):
    t = pl.program_id(1)

    @pl.when(t == 0)
    def _():
        st_scr[...] = jnp.zeros(st_scr.shape, F32)

    nc = GLA_ROWS // GLA_CHUNK
    pre = jnp.dot(g_ref[...].astype(BF16), wg_ref[...], preferred_element_type=F32) + bg_ref[...]
    log_a = (jnp.minimum(pre, 0.0) - jnp.log(1.0 + jnp.exp(-jnp.abs(pre)))) * (1.0 / GLA_GATE_TAU)
    hi = log_a.astype(BF16)
    lo = (log_a - hi.astype(F32)).astype(BF16)
    tri = tri_ref[...]
    bcum = (jnp.dot(tri, hi, preferred_element_type=F32)
            + jnp.dot(tri, lo, preferred_element_type=F32))

    row = lax.broadcasted_iota(jnp.int32, (GLA_ROWS, GLA_ROWS), 0)
    col = lax.broadcasted_iota(jnp.int32, (GLA_ROWS, GLA_ROWS), 1)
    causal = col <= row
    nw = nw_ref[...]
    nt_dims = (((1,), (1,)), ((), ()))

    for h in range(GLA_HEADS):
        ks = slice(h * GLA_HEAD_K, (h + 1) * GLA_HEAD_K)
        vs = slice(h * GLA_HEAD_V, (h + 1) * GLA_HEAD_V)
        bh = bcum[:, ks]
        b_end = [bh[(i + 1) * GLA_CHUNK - 1:(i + 1) * GLA_CHUNK, :] for i in range(nc)]
        q_dec, q_int, k_inv, k_end, k_st = [], [], [], [], []
        for i in range(nc):
            rows = slice(i * GLA_CHUNK, (i + 1) * GLA_CHUNK)
            b = bh[rows, :] if i == 0 else bh[rows, :] - b_end[i - 1]
            qd = q_ref[rows, ks].astype(F32) * (GLA_HEAD_K ** -0.5) * jnp.exp(b)
            ki = k_ref[rows, ks].astype(F32) * jnp.exp(-b)
            ke = ki * jnp.exp(b_end[i] if i == 0 else b_end[i] - b_end[i - 1])
            q_dec.append(qd.astype(BF16))
            q_int.append((qd if i == 0 else qd * jnp.exp(b_end[i - 1])).astype(BF16))
            k_inv.append(ki.astype(BF16))
            k_end.append(ke)
            k_st.append((ke if i == nc - 1 else ke * jnp.exp(b_end[nc - 1] - b_end[i]))
                        .astype(BF16))
        attn_rows = []
        for i in range(nc):
            pieces = []
            for j in range(i):
                kj = k_end[j] if j == i - 1 else k_end[j] * jnp.exp(b_end[i - 1] - b_end[j])
                pieces.append(kj.astype(BF16))
            pieces.append(k_inv[i])
            if i < nc - 1:
                pieces.append(jnp.zeros(((nc - 1 - i) * GLA_CHUNK, GLA_HEAD_K), BF16))
            attn_rows.append(lax.dot_general(q_dec[i], jnp.concatenate(pieces, axis=0), nt_dims,
                                             preferred_element_type=F32))
        attn = jnp.where(causal, jnp.concatenate(attn_rows, axis=0), 0.0).astype(BF16)
        st = st_scr[h]
        v = v_ref[:, vs]
        o = jnp.dot(jnp.concatenate([attn] + [jnp.concatenate(q_int, axis=0)], axis=1),
                    jnp.concatenate([v, st.astype(BF16)], axis=0),
                    preferred_element_type=F32)
        dec_col = jnp.broadcast_to(jnp.exp(b_end[nc - 1]), (LANES, GLA_HEAD_K)).T
        dec = jnp.concatenate([dec_col] * (GLA_HEAD_V // LANES), axis=1)
        st_scr[h] = st * dec + lax.dot_general(
            jnp.concatenate(k_st, axis=0), v, (((0,), (0,)), ((), ())),
            preferred_element_type=F32)
        ms = jnp.mean(o * o, axis=-1, keepdims=True)
        z = z_ref[:, vs].astype(F32)
        o_ref[:, vs] = (o * lax.rsqrt(ms + EPS) * nw * (z * jax.nn.sigmoid(z))).astype(BF16)


def _gla(main, glow, wg_bf, bg, tri, nw):
    nt = SEQ // GLA_ROWS
    assert 2 * GLA_DK == D_INNER
    return pl.pallas_call(
        _gla_kernel,
        grid=(BATCH, nt),
        in_specs=[
            pl.BlockSpec((GLA_ROWS, GLA_DK), lambda b, t: (b * nt + t, 0)),
            pl.BlockSpec((GLA_ROWS, GLA_DK), lambda b, t: (b * nt + t, 1)),
            pl.BlockSpec((GLA_ROWS, D_INNER), lambda b, t: (b * nt + t, 1)),
            pl.BlockSpec((GLA_ROWS, D_INNER), lambda b, t: (b * nt + t, 2)),
            pl.BlockSpec((GLA_ROWS, LANES), lambda b, t: (b * nt + t, 0)),
            pl.BlockSpec((LANES, GLA_DK), lambda b, t: (0, 0)),
            pl.BlockSpec((1, GLA_DK), lambda b, t: (0, 0)),
            pl.BlockSpec((GLA_ROWS, GLA_ROWS), lambda b, t: (0, 0)),
            pl.BlockSpec((1, GLA_HEAD_V), lambda b, t: (0, 0)),
        ],
        out_specs=pl.BlockSpec((GLA_ROWS, D_INNER), lambda b, t: (b * nt + t, 0)),
        out_shape=jax.ShapeDtypeStruct((M_ROWS, D_INNER), BF16),
        scratch_shapes=[pltpu.VMEM((GLA_HEADS, GLA_HEAD_K, GLA_HEAD_V), F32)],
        compiler_params=_cparams(("parallel", "arbitrary")),
    )(main, main, main, main, glow, wg_bf, bg, tri, nw)


def _permute_qk_columns(w_in):
    d = w_in.shape[0]
    qk = w_in[:, :2 * D_INNER].reshape(d, 2 * DA_HEADS, 4, ROPE_HALF)
    qk = jnp.stack([qk[:, :, 0], qk[:, :, 2], qk[:, :, 1], qk[:, :, 3]], axis=2)
    return jnp.concatenate([qk.reshape(d, 2 * D_INNER), w_in[:, 2 * D_INNER:]], axis=1)


def kernel(x, c, positions, ada_w, ada_b, norm_g, da_w_in, da_lam_q1, da_lam_k1, da_lam_q2,
           da_lam_k2, da_subln_w, da_w_out, gla_w_in, gla_w_gate_up, gla_b_gate, gla_norm_w,
           gla_w_out, final_g):
    x2 = x.reshape(M_ROWS, D_MODEL)
    c_pad = jnp.pad(c, ((0, ADA_ROWS - BATCH), (0, 0))).astype(BF16)
    mod = _ada(c_pad, ada_w, ada_b.reshape(DEPTH, 1, 3 * D_MODEL))
    mod = mod.reshape(DEPTH * ADA_ROWS * 3, 1, D_MODEL)
    norm_g3 = norm_g.reshape(DEPTH, 1, D_MODEL)

    inv_freq = ROPE_THETA ** (-jnp.arange(0, DA_HEAD_DIM, 2, dtype=F32) / DA_HEAD_DIM)
    cos, sin = _rope_tables(positions.reshape(BATCH, 1, SEQ), inv_freq.reshape(ROPE_HALF, 1))

    lambda_init = 0.8 - 0.6 * math.exp(-0.3 * 0)
    qkvz = _inproj_da(x2, norm_g3, mod, _permute_qk_columns(da_w_in[0]).astype(BF16), cos, sin,
                      layer=0)
    lamv = jnp.concatenate([da_lam_q1, da_lam_k1, da_lam_q2, da_lam_k2], axis=0)
    lamv = jnp.pad(lamv, ((0, 4), (0, LANES - DA_HEAD_DIM)))
    o = _diff_attn(qkvz, lamv, da_subln_w[0].reshape(DA_V_DIM, 1), lambda_init)
    x2 = _outproj(o, da_w_out[0].astype(BF16), x2, mod, final_g.reshape(1, D_MODEL),
                  layer=0, final_norm=False)

    w = gla_w_in[0]
    wg_in = jnp.pad(w[:, GLA_MAIN:], ((0, 0), (0, LANES - GLA_GATE_RANK))).astype(BF16)
    main, glow = _inproj_gla(x2, norm_g3, mod, w.astype(BF16), wg_in, layer=1)
    wg_up = jnp.pad(gla_w_gate_up[0], ((0, LANES - GLA_GATE_RANK), (0, 0))).astype(BF16)
    r = jnp.arange(GLA_ROWS)
    tri = (r[:, None] >= r[None, :]).astype(BF16)
    o = _gla(main, glow, wg_up, gla_b_gate, tri, gla_norm_w)
    out = _outproj(o, gla_w_out[0].astype(BF16), x2, mod, final_g.reshape(1, D_MODEL),
                   layer=1, final_norm=True)
    return out.reshape(BATCH, SEQ, D_MODEL)
```

```python
import functools
import math

import jax
import jax.numpy as jnp
from jax import lax
from jax.experimental import pallas as pl
from jax.experimental.pallas import tpu as pltpu

F32 = jnp.float32
BF16 = jnp.bfloat16

D_MODEL = 1024
BATCH = 4
SEQ = 4096
DEPTH = 2
D_INNER = 2 * D_MODEL
EPS = 1e-6
M_ROWS = BATCH * SEQ

DA_HEADS = 16
DA_HEAD_DIM = 64
DA_V_DIM = 128
ROPE_THETA = 10000.0
ROPE_HALF = DA_HEAD_DIM // 2

GLA_HEADS = 4
GLA_DK = 1024
GLA_HEAD_K = 256
GLA_HEAD_V = 512
GLA_GATE_RANK = 16
GLA_GATE_TAU = 16.0
GLA_CHUNK = 64
GLA_MAIN = 2 * GLA_DK + 2 * D_INNER

LANES = 128
ADA_ROWS = 16
VMEM_LIMIT = 56 * 1024 * 1024

TM_IN, TN_IN = 2048, 1024
RC_IN = 256
RC_OUT = 128
TM_OUT = 512
TQ = 1024
TK = 256
NSUB = TQ // TK
CB = 256
assert TQ % TK == 0 and TQ % CB == 0
ONES_ROWS = 16
GLA_ROWS = 256
TM_ROPE = 1024
TN_ADA = 1536

NEG = -1e30


def _cparams(sem):
    return pltpu.CompilerParams(dimension_semantics=sem, vmem_limit_bytes=VMEM_LIMIT)


def _ada_kernel(c_ref, w_ref, b_ref, o_ref):
    o_ref[...] = jnp.dot(c_ref[...], w_ref[...].astype(BF16),
                         preferred_element_type=F32) + b_ref[...]


def _ada(c_pad, ada_w, ada_b3):
    n = 3 * D_MODEL
    return pl.pallas_call(
        _ada_kernel,
        grid=(DEPTH, n // TN_ADA),
        in_specs=[
            pl.BlockSpec((ADA_ROWS, D_MODEL), lambda l, j: (0, 0)),
            pl.BlockSpec((None, D_MODEL, TN_ADA), lambda l, j: (l, 0, j)),
            pl.BlockSpec((None, 1, TN_ADA), lambda l, j: (l, 0, j)),
        ],
        out_specs=pl.BlockSpec((None, ADA_ROWS, TN_ADA), lambda l, j: (l, 0, j)),
        out_shape=jax.ShapeDtypeStruct((DEPTH, ADA_ROWS, n), F32),
        compiler_params=_cparams(("parallel", "parallel")),
    )(c_pad, ada_w, ada_b3)


def _rope_kernel(pos_ref, invf_ref, cos_ref, sin_ref):
    ang = invf_ref[...] * pos_ref[...].astype(F32)
    c = jnp.cos(ang)
    s = jnp.sin(ang)
    cos_ref[...] = jnp.concatenate([c, c, c, c], axis=0).T
    sin_ref[...] = jnp.concatenate([-s, -s, s, s], axis=0).T


def _rope_tables(pos3, invf):
    nt = SEQ // TM_ROPE
    return pl.pallas_call(
        _rope_kernel,
        grid=(BATCH, nt),
        in_specs=[
            pl.BlockSpec((None, 1, TM_ROPE), lambda b, t: (b, 0, t)),
            pl.BlockSpec((ROPE_HALF, 1), lambda b, t: (0, 0)),
        ],
        out_specs=[
            pl.BlockSpec((TM_ROPE, LANES), lambda b, t: (b * nt + t, 0)),
            pl.BlockSpec((TM_ROPE, LANES), lambda b, t: (b * nt + t, 0)),
        ],
        out_shape=[jax.ShapeDtypeStruct((M_ROWS, LANES), F32)] * 2,
        compiler_params=_cparams(("parallel", "parallel")),
    )(pos3, invf)


def _pipeline_row_chunks(n_chunks, chunk_dot, chunk_store):
    acc = chunk_dot(0)
    for r in range(n_chunks):
        nxt = chunk_dot(r + 1) if r + 1 < n_chunks else None
        chunk_store(r, acc)
        acc = nxt


def _norm_modulate(x, g, scale, shift):
    ms = jnp.mean(x * x, axis=-1, keepdims=True)
    return x * lax.rsqrt(ms + EPS) * (g * (1.0 + scale)) + shift


def _inproj_da_kernel(x_ref, g_ref, scale_ref, shift_ref, w_ref, cos_ref, sin_ref,
                      o_ref, h_scr, *, n_q_tiles, n_rope_tiles, q_scale):
    j = pl.program_id(1)

    @pl.when(j == 0)
    def _():
        for r in range(TM_IN // RC_IN):
            rows = slice(r * RC_IN, (r + 1) * RC_IN)
            h_scr[rows, :] = _norm_modulate(x_ref[rows, :], g_ref[...], scale_ref[...],
                                            shift_ref[...]).astype(BF16)

    is_rope = j < n_rope_tiles
    sc = jnp.where(j < n_q_tiles, q_scale, 1.0).astype(F32)
    cos = jnp.where(is_rope, cos_ref[...] * sc, 1.0)
    sin = jnp.where(is_rope, sin_ref[...] * sc, 0.0)
    n_chunks = TM_IN // RC_IN

    def chunk_dot(r):
        return jnp.dot(h_scr[r * RC_IN:(r + 1) * RC_IN, :], w_ref[...],
                       preferred_element_type=F32)

    def chunk_store(r, acc):
        rows = slice(r * RC_IN, (r + 1) * RC_IN)
        for gidx in range(acc.shape[1] // LANES):
            cols = slice(gidx * LANES, (gidx + 1) * LANES)
            t = acc[:, cols]
            o_ref[rows, cols] = (t * cos[rows, :]
                                 + pltpu.roll(t, LANES // 2, 1) * sin[rows, :]).astype(BF16)

    _pipeline_row_chunks(n_chunks, chunk_dot, chunk_store)


def _inproj_da(x2, norm_g, mod, w_bf, cos, sin, layer):
    n = w_bf.shape[1]
    seg_tiles = D_INNER // TN_IN
    tiles_per_batch = SEQ // TM_IN

    def mod_idx(which):
        return lambda i, j: ((layer * ADA_ROWS + i // tiles_per_batch) * 3 + which, 0, 0)

    kern = functools.partial(_inproj_da_kernel, n_q_tiles=seg_tiles,
                             n_rope_tiles=2 * seg_tiles,
                             q_scale=DA_HEAD_DIM ** -0.5 * math.log2(math.e))
    return pl.pallas_call(
        kern,
        grid=(M_ROWS // TM_IN, n // TN_IN),
        in_specs=[
            pl.BlockSpec((TM_IN, D_MODEL), lambda i, j: (i, 0)),
            pl.BlockSpec((None, 1, D_MODEL), lambda i, j: (layer, 0, 0)),
            pl.BlockSpec((None, 1, D_MODEL), mod_idx(1)),
            pl.BlockSpec((None, 1, D_MODEL), mod_idx(0)),
            pl.BlockSpec((D_MODEL, TN_IN), lambda i, j: (0, j)),
            pl.BlockSpec((TM_IN, LANES), lambda i, j: (i, 0)),
            pl.BlockSpec((TM_IN, LANES), lambda i, j: (i, 0)),
        ],
        out_specs=pl.BlockSpec((None, TM_IN, TN_IN),
                               lambda i, j: (j // seg_tiles, i, j % seg_tiles)),
        out_shape=jax.ShapeDtypeStruct((4, M_ROWS, D_INNER), BF16),
        scratch_shapes=[pltpu.VMEM((TM_IN, D_MODEL), BF16)],
        compiler_params=_cparams(("parallel", "arbitrary")),
    )(x2, norm_g, mod, mod, w_bf, cos, sin)


def _inproj_gla_kernel(x_ref, g_ref, scale_ref, shift_ref, w_ref, wg_ref,
                       o_ref, glow_ref, h_scr):
    j = pl.program_id(1)

    @pl.when(j == 0)
    def _():
        for r in range(TM_IN // RC_IN):
            rows = slice(r * RC_IN, (r + 1) * RC_IN)
            h = _norm_modulate(x_ref[rows, :], g_ref[...], scale_ref[...],
                               shift_ref[...]).astype(BF16)
            h_scr[rows, :] = h
            glow_ref[rows, :] = jnp.dot(h, wg_ref[...], preferred_element_type=F32)

    def chunk_dot(r):
        return jnp.dot(h_scr[r * RC_IN:(r + 1) * RC_IN, :], w_ref[...],
                       preferred_element_type=F32)

    def chunk_store(r, acc):
        o_ref[r * RC_IN:(r + 1) * RC_IN, :] = acc.astype(BF16)

    _pipeline_row_chunks(TM_IN // RC_IN, chunk_dot, chunk_store)


def _inproj_gla(x2, norm_g, mod, w_bf, wg_bf, layer):
    n = GLA_MAIN
    tiles_per_batch = SEQ // TM_IN

    def mod_idx(which):
        return lambda i, j: ((layer * ADA_ROWS + i // tiles_per_batch) * 3 + which, 0, 0)

    return pl.pallas_call(
        _inproj_gla_kernel,
        grid=(M_ROWS // TM_IN, n // TN_IN),
        in_specs=[
            pl.BlockSpec((TM_IN, D_MODEL), lambda i, j: (i, 0)),
            pl.BlockSpec((None, 1, D_MODEL), lambda i, j: (layer, 0, 0)),
            pl.BlockSpec((None, 1, D_MODEL), mod_idx(1)),
            pl.BlockSpec((None, 1, D_MODEL), mod_idx(0)),
            pl.BlockSpec((D_MODEL, TN_IN), lambda i, j: (0, j)),
            pl.BlockSpec((D_MODEL, LANES), lambda i, j: (0, 0)),
        ],
        out_specs=[
            pl.BlockSpec((TM_IN, TN_IN), lambda i, j: (i, j)),
            pl.BlockSpec((TM_IN, LANES), lambda i, j: (i, 0)),
        ],
        out_shape=[jax.ShapeDtypeStruct((M_ROWS, n), BF16),
                   jax.ShapeDtypeStruct((M_ROWS, LANES), F32)],
        scratch_shapes=[pltpu.VMEM((TM_IN, D_MODEL), BF16)],
        compiler_params=_cparams(("parallel", "arbitrary")),
    )(x2, norm_g, mod, mod, w_bf, wg_bf)


def _outproj_kernel(o_ref, w_ref, x_ref, gate_ref, fg_ref, out_ref, *, final_norm):
    def chunk_dot(r):
        return jnp.dot(o_ref[r * RC_OUT:(r + 1) * RC_OUT, :], w_ref[...],
                       preferred_element_type=F32)

    def chunk_store(r, y):
        rows = slice(r * RC_OUT, (r + 1) * RC_OUT)
        x = x_ref[rows, :] + gate_ref[...] * y
        if final_norm:
            ms = jnp.mean(x * x, axis=-1, keepdims=True)
            x = x * lax.rsqrt(ms + EPS) * fg_ref[...]
        out_ref[rows, :] = x

    _pipeline_row_chunks(TM_OUT // RC_OUT, chunk_dot, chunk_store)


def _outproj(o, w_bf, x2, mod, final_g, layer, final_norm):
    tiles_per_batch = SEQ // TM_OUT
    kern = functools.partial(_outproj_kernel, final_norm=final_norm)
    return pl.pallas_call(
        kern,
        grid=(M_ROWS // TM_OUT,),
        in_specs=[
            pl.BlockSpec((TM_OUT, D_INNER), lambda i: (i, 0)),
            pl.BlockSpec((D_INNER, D_MODEL), lambda i: (0, 0)),
            pl.BlockSpec((TM_OUT, D_MODEL), lambda i: (i, 0)),
            pl.BlockSpec((None, 1, D_MODEL),
                         lambda i: ((layer * ADA_ROWS + i // tiles_per_batch) * 3 + 2, 0, 0)),
            pl.BlockSpec((1, D_MODEL), lambda i: (0, 0)),
        ],
        out_specs=pl.BlockSpec((TM_OUT, D_MODEL), lambda i: (i, 0)),
        out_shape=jax.ShapeDtypeStruct((M_ROWS, D_MODEL), F32),
        compiler_params=_cparams(("parallel",)),
    )(o, w_bf, x2, mod, final_g)


def _aligned(start, multiple):
    return start if isinstance(start, int) else pl.multiple_of(start, multiple)


def _diff_attn_tile(qi, acc, finish_previous, q_ref, k_ref, vt_scr, m_scr, p_scr, alpha_scr):
    q = q_ref[pl.ds(_aligned(qi * TQ, TQ), TQ), :]
    lane = lax.broadcasted_iota(jnp.int32, q.shape, 1)
    zero = jnp.zeros_like(q)
    is_a = (lane % DA_HEAD_DIM) < ROPE_HALF
    qs = jnp.concatenate([jnp.where(is_a, q, zero), jnp.where(is_a, zero, q)],
                         axis=0)

    m_scr[...] = jnp.full(m_scr.shape, NEG, F32)
    acc[...] = jnp.zeros(acc.shape, F32)

    def deferred_pv(prev_start, cs):
        vt = vt_scr[:, pl.ds(_aligned(prev_start, TK), TK)]
        acc[:, cs] = alpha_scr[:, cs] * acc[:, cs] + jnp.dot(
            vt, p_scr[:, cs], preferred_element_type=F32)

    def block_masked_out(diag_offset, c):
        return diag_offset is not None and diag_offset > (c * CB) % TQ + CB - 1

    def key_round(start, diag_offset, prev_start, prev_diag_offset=None):
        k = k_ref[pl.ds(_aligned(start, TK), TK), :]
        for c in range(2 * TQ // CB):
            cs = slice(c * CB, (c + 1) * CB)
            q_lo = (c * CB) % TQ
            masked_out = block_masked_out(diag_offset, c)
            needs_mask = (diag_offset is not None and not masked_out
                          and diag_offset + TK - 1 > q_lo)
            if not masked_out:
                st = lax.dot_general(k, qs[cs], (((1,), (1,)), ((), ())),
                                     preferred_element_type=F32)
            if prev_start is not None and not block_masked_out(prev_diag_offset, c):
                deferred_pv(prev_start, cs)
            if masked_out:
                alpha_scr[:, cs] = jnp.ones((1, CB), F32)
                p_scr[:, cs] = jnp.zeros((TK, CB), BF16)
                continue
            if needs_mask:
                kpos = lax.broadcasted_iota(jnp.int32, st.shape, 0) + diag_offset
                qpos = lax.broadcasted_iota(jnp.int32, st.shape, 1) + q_lo
                st = jnp.where(kpos <= qpos, st, NEG)
            m_prev = m_scr[:, cs]
            m_new = jnp.maximum(m_prev, jnp.max(st, axis=0, keepdims=True))
            alpha_scr[:, cs] = jnp.exp2(m_prev - m_new)
            p_scr[:, cs] = jnp.exp2(st - m_new).astype(BF16)
            m_scr[:, cs] = m_new

    diag = qi * TQ
    last_sub = (NSUB - 1) * TK
    key_round(diag, 0, None)
    for u in range(1, NSUB):
        key_round(diag + u * TK, u * TK, diag + (u - 1) * TK, (u - 1) * TK)

    def body(kc, carry):
        base = kc * TQ
        key_round(base, None, jnp.where(kc == 0, diag + last_sub, base - TK))
        for u in range(1, NSUB):
            key_round(base + u * TK, None, base + (u - 1) * TK)
        return carry

    finish_previous()
    if not (isinstance(qi, int) and qi == 0):
        lax.fori_loop(0, qi, body, 0)
    last_start = last_sub if isinstance(qi, int) and qi == 0 else diag - TK
    for c in range(2 * TQ // CB):
        deferred_pv(last_start, slice(c * CB, (c + 1) * CB))


def _diff_attn_kernel(q_ref, k_ref, v_ref, z_ref, lamv_ref, subw_ref, o_ref,
                      vt_scr, m_scr, acc_scr, p_scr, alpha_scr, *, lambda_init):
    nq = SEQ // TQ
    vt_scr[0:DA_V_DIM, :] = v_ref[...].astype(F32).T.astype(BF16)
    vt_scr[DA_V_DIM:, :] = jnp.ones((ONES_ROWS, SEQ), BF16)
    lamv = lamv_ref[...]
    lam = (jnp.exp(jnp.sum(lamv[0:1, :] * lamv[1:2, :], keepdims=True))
           - jnp.exp(jnp.sum(lamv[2:3, :] * lamv[3:4, :], keepdims=True)) + lambda_init)
    subw = subw_ref[...] * (1.0 - lambda_init)

    def epilogue(qi, acc):
        rows = pl.ds(_aligned(qi * TQ, TQ), TQ)
        a = acc[...]
        on = a[0:DA_V_DIM, :] / a[DA_V_DIM:DA_V_DIM + 1, :]
        ot = on[:, :TQ] - lam * on[:, TQ:]
        ms = jnp.mean(ot * ot, axis=0, keepdims=True)
        ot = ot * lax.rsqrt(ms + EPS) * subw
        z = z_ref[rows, :].astype(F32)
        o_ref[rows, :] = (ot.T * (z * jax.nn.sigmoid(z))).astype(BF16)

    tile = functools.partial(_diff_attn_tile, q_ref=q_ref, k_ref=k_ref, vt_scr=vt_scr,
                             m_scr=m_scr, p_scr=p_scr, alpha_scr=alpha_scr)
    tile(0, acc_scr.at[0], lambda: None)

    def tile_body(qi, carry):
        slot = qi % 2
        tile(qi, acc_scr.at[slot], lambda: epilogue(qi - 1, acc_scr.at[1 - slot]))
        return carry

    lax.fori_loop(1, nq, tile_body, 0)
    epilogue(nq - 1, acc_scr.at[(nq - 1) % 2])


def _diff_attn(qkvz, lamv, subw_col, lambda_init):
    kern = functools.partial(_diff_attn_kernel, lambda_init=lambda_init)
    head_block = lambda which: pl.BlockSpec((None, SEQ, LANES), lambda b, h: (which, b, h))
    return pl.pallas_call(
        kern,
        grid=(BATCH, DA_HEADS),
        in_specs=[
            head_block(0), head_block(1), head_block(2), head_block(3),
            pl.BlockSpec((8, LANES), lambda b, h: (0, 0)),
            pl.BlockSpec((DA_V_DIM, 1), lambda b, h: (0, 0)),
        ],
        out_specs=pl.BlockSpec((SEQ, LANES), lambda b, h: (b, h)),
        out_shape=jax.ShapeDtypeStruct((M_ROWS, D_INNER), BF16),
        scratch_shapes=[
            pltpu.VMEM((DA_V_DIM + ONES_ROWS, SEQ), BF16),
            pltpu.VMEM((1, 2 * TQ), F32),
            pltpu.VMEM((2, DA_V_DIM + ONES_ROWS, 2 * TQ), F32),
            pltpu.VMEM((TK, 2 * TQ), BF16),
            pltpu.VMEM((1, 2 * TQ), F32),
        ],
        compiler_params=_cparams(("parallel", "parallel")),
    )(qkvz, qkvz, qkvz, qkvz, lamv, subw_col)


def _gla_kernel(q_ref, k_ref, v_ref, z_ref, g_ref, wg_ref, bg_ref, tri_ref, nw_ref,
                o_ref, st_scr):
    t = pl.program_id(1)

    @pl.when(t == 0)
    def _():
        st_scr[...] = jnp.zeros(st_scr.shape, F32)

    nc = GLA_ROWS // GLA_CHUNK
    pre = jnp.dot(g_ref[...].astype(BF16), wg_ref[...], preferred_element_type=F32) + bg_ref[...]
    log_a = (jnp.minimum(pre, 0.0) - jnp.log(1.0 + jnp.exp(-jnp.abs(pre)))) * (1.0 / GLA_GATE_TAU)
    hi = log_a.astype(BF16)
    lo = (log_a - hi.astype(F32)).astype(BF16)
    tri = tri_ref[...]
    bcum = (jnp.dot(tri, hi, preferred_element_type=F32)
            + jnp.dot(tri, lo, preferred_element_type=F32))

    row = lax.broadcasted_iota(jnp.int32, (GLA_ROWS, GLA_ROWS), 0)
    col = lax.broadcasted_iota(jnp.int32, (GLA_ROWS, GLA_ROWS), 1)
    causal = col <= row
    nw = nw_ref[...]
    nt_dims = (((1,), (1,)), ((), ()))

    for h in range(GLA_HEADS):
        ks = slice(h * GLA_HEAD_K, (h + 1) * GLA_HEAD_K)
        vs = slice(h * GLA_HEAD_V, (h + 1) * GLA_HEAD_V)
        bh = bcum[:, ks]
        b_end = [bh[(i + 1) * GLA_CHUNK - 1:(i + 1) * GLA_CHUNK, :] for i in range(nc)]
        q_dec, q_int, k_inv, k_end, k_st = [], [], [], [], []
        for i in range(nc):
            rows = slice(i * GLA_CHUNK, (i + 1) * GLA_CHUNK)
            b = bh[rows, :] if i == 0 else bh[rows, :] - b_end[i - 1]
            qd = q_ref[rows, ks].astype(F32) * (GLA_HEAD_K ** -0.5) * jnp.exp(b)
            ki = k_ref[rows, ks].astype(F32) * jnp.exp(-b)
            ke = ki * jnp.exp(b_end[i] if i == 0 else b_end[i] - b_end[i - 1])
            q_dec.append(qd.astype(BF16))
            q_int.append((qd if i == 0 else qd * jnp.exp(b_end[i - 1])).astype(BF16))
            k_inv.append(ki.astype(BF16))
            k_end.append(ke)
            k_st.append((ke if i == nc - 1 else ke * jnp.exp(b_end[nc - 1] - b_end[i]))
                        .astype(BF16))
        attn_rows = []
        for i in range(nc):
            pieces = []
            for j in range(i):
                kj = k_end[j] if j == i - 1 else k_end[j] * jnp.exp(b_end[i - 1] - b_end[j])
                pieces.append(kj.astype(BF16))
            pieces.append(k_inv[i])
            if i < nc - 1:
                pieces.append(jnp.zeros(((nc - 1 - i) * GLA_CHUNK, GLA_HEAD_K), BF16))
            attn_rows.append(lax.dot_general(q_dec[i], jnp.concatenate(pieces, axis=0), nt_dims,
                                             preferred_element_type=F32))
        attn = jnp.where(causal, jnp.concatenate(attn_rows, axis=0), 0.0).astype(BF16)
        st = st_scr[h]
        v = v_ref[:, vs]
        o = jnp.dot(jnp.concatenate([attn] + [jnp.concatenate(q_int, axis=0)], axis=1),
                    jnp.concatenate([v, st.astype(BF16)], axis=0),
                    preferred_element_type=F32)
        dec_col = jnp.broadcast_to(jnp.exp(b_end[nc - 1]), (LANES, GLA_HEAD_K)).T
        dec = jnp.concatenate([dec_col] * (GLA_HEAD_V // LANES), axis=1)
        st_scr[h] = st * dec + lax.dot_general(
            jnp.concatenate(k_st, axis=0), v, (((0,), (0,)), ((), ())),
            preferred_element_type=F32)
        ms = jnp.mean(o * o, axis=-1, keepdims=True)
        z = z_ref[:, vs].astype(F32)
        o_ref[:, vs] = (o * lax.rsqrt(ms + EPS) * nw * (z * jax.nn.sigmoid(z))).astype(BF16)


def _gla(main, glow, wg_bf, bg, tri, nw):
    nt = SEQ // GLA_ROWS
    assert 2 * GLA_DK == D_INNER
    return pl.pallas_call(
        _gla_kernel,
        grid=(BATCH, nt),
        in_specs=[
            pl.BlockSpec((GLA_ROWS, GLA_DK), lambda b, t: (b * nt + t, 0)),
            pl.BlockSpec((GLA_ROWS, GLA_DK), lambda b, t: (b * nt + t, 1)),
            pl.BlockSpec((GLA_ROWS, D_INNER), lambda b, t: (b * nt + t, 1)),
            pl.BlockSpec((GLA_ROWS, D_INNER), lambda b, t: (b * nt + t, 2)),
            pl.BlockSpec((GLA_ROWS, LANES), lambda b, t: (b * nt + t, 0)),
            pl.BlockSpec((LANES, GLA_DK), lambda b, t: (0, 0)),
            pl.BlockSpec((1, GLA_DK), lambda b, t: (0, 0)),
            pl.BlockSpec((GLA_ROWS, GLA_ROWS), lambda b, t: (0, 0)),
            pl.BlockSpec((1, GLA_HEAD_V), lambda b, t: (0, 0)),
        ],
        out_specs=pl.BlockSpec((GLA_ROWS, D_INNER), lambda b, t: (b * nt + t, 0)),
        out_shape=jax.ShapeDtypeStruct((M_ROWS, D_INNER), BF16),
        scratch_shapes=[pltpu.VMEM((GLA_HEADS, GLA_HEAD_K, GLA_HEAD_V), F32)],
        compiler_params=_cparams(("parallel", "arbitrary")),
    )(main, main, main, main, glow, wg_bf, bg, tri, nw)


def _prep_da_w_kernel(w_ref, o_ref, *, n_qk_tiles):
    j = pl.program_id(0)

    @pl.when(j < n_qk_tiles)
    def _():
        lane = lax.broadcasted_iota(jnp.int32, (w_ref.shape[0], LANES), 1)
        take_next = (lane >= ROPE_HALF) & (lane < 2 * ROPE_HALF)
        take_prev = (lane >= 2 * ROPE_HALF) & (lane < 3 * ROPE_HALF)
        for gidx in range(w_ref.shape[1] // LANES):
            cols = slice(gidx * LANES, (gidx + 1) * LANES)
            w = w_ref[:, cols]
            nxt = pltpu.roll(w, LANES - ROPE_HALF, 1)
            prv = pltpu.roll(w, ROPE_HALF, 1)
            o_ref[:, cols] = jnp.where(take_next, nxt,
                                       jnp.where(take_prev, prv, w)).astype(BF16)

    @pl.when(j >= n_qk_tiles)
    def _():
        o_ref[...] = w_ref[...].astype(BF16)


def _prep_da_w(w_in):
    d, n = w_in.shape
    kern = functools.partial(_prep_da_w_kernel, n_qk_tiles=2 * D_INNER // TN_IN)
    return pl.pallas_call(
        kern,
        grid=(n // TN_IN,),
        in_specs=[pl.BlockSpec((d, TN_IN), lambda j: (0, j))],
        out_specs=pl.BlockSpec((d, TN_IN), lambda j: (0, j)),
        out_shape=jax.ShapeDtypeStruct((d, n), BF16),
        compiler_params=_cparams(("parallel",)),
    )(w_in)


def kernel(x, c, positions, ada_w, ada_b, norm_g, da_w_in, da_lam_q1, da_lam_k1, da_lam_q2,
           da_lam_k2, da_subln_w, da_w_out, gla_w_in, gla_w_gate_up, gla_b_gate, gla_norm_w,
           gla_w_out, final_g):
    x2 = x.reshape(M_ROWS, D_MODEL)
    c_pad = jnp.pad(c, ((0, ADA_ROWS - BATCH), (0, 0))).astype(BF16)
    mod = _ada(c_pad, ada_w, ada_b.reshape(DEPTH, 1, 3 * D_MODEL))
    mod = mod.reshape(DEPTH * ADA_ROWS * 3, 1, D_MODEL)
    norm_g3 = norm_g.reshape(DEPTH, 1, D_MODEL)

    inv_freq = ROPE_THETA ** (-jnp.arange(0, DA_HEAD_DIM, 2, dtype=F32) / DA_HEAD_DIM)
    cos, sin = _rope_tables(positions.reshape(BATCH, 1, SEQ), inv_freq.reshape(ROPE_HALF, 1))

    lambda_init = 0.8 - 0.6 * math.exp(-0.3 * 0)
    qkvz = _inproj_da(x2, norm_g3, mod, _prep_da_w(da_w_in[0]), cos, sin, layer=0)
    lamv = jnp.concatenate([da_lam_q1, da_lam_k1, da_lam_q2, da_lam_k2], axis=0)
    lamv = jnp.pad(lamv, ((0, 4), (0, LANES - DA_HEAD_DIM)))
    o = _diff_attn(qkvz, lamv, da_subln_w[0].reshape(DA_V_DIM, 1), lambda_init)
    x2 = _outproj(o, da_w_out[0].astype(BF16), x2, mod, final_g.reshape(1, D_MODEL),
                  layer=0, final_norm=False)

    w = gla_w_in[0]
    wg_in = jnp.pad(w[:, GLA_MAIN:], ((0, 0), (0, LANES - GLA_GATE_RANK))).astype(BF16)
    main, glow = _inproj_gla(x2, norm_g3, mod, w.astype(BF16), wg_in, layer=1)
    wg_up = jnp.pad(gla_w_gate_up[0], ((0, LANES - GLA_GATE_RANK), (0, 0))).astype(BF16)
    r = jnp.arange(GLA_ROWS)
    tri = (r[:, None] >= r[None, :]).astype(BF16)
    o = _gla(main, glow, wg_up, gla_b_gate, tri, gla_norm_w)
    out = _outproj(o, gla_w_out[0].astype(BF16), x2, mod, final_g.reshape(1, D_MODEL),
                   layer=1, final_norm=True)
    return out.reshape(BATCH, SEQ, D_MODEL)
```

```python
import functools
import math

import jax
import jax.numpy as jnp
from jax import lax
from jax.experimental import pallas as pl
from jax.experimental.pallas import tpu as pltpu

F32 = jnp.float32
BF16 = jnp.bfloat16

D_MODEL = 1024
BATCH = 4
SEQ = 4096
DEPTH = 2
D_INNER = 2 * D_MODEL
EPS = 1e-6
M_ROWS = BATCH * SEQ

DA_HEADS = 16
DA_HEAD_DIM = 64
DA_V_DIM = 128
ROPE_THETA = 10000.0
ROPE_HALF = DA_HEAD_DIM // 2

GLA_HEADS = 4
GLA_DK = 1024
GLA_HEAD_K = 256
GLA_HEAD_V = 512
GLA_GATE_RANK = 16
GLA_GATE_TAU = 16.0
GLA_CHUNK = 64
GLA_MAIN = 2 * GLA_DK + 2 * D_INNER

LANES = 128
ADA_ROWS = 16
VMEM_LIMIT = 56 * 1024 * 1024

TM_IN, TN_IN = 2048, 1024
RC_IN = 256
RC_OUT = 128
TM_OUT = 512
TQ = 1024
TK = 256
NSUB = TQ // TK
CB = 256
assert TQ % TK == 0 and TQ % CB == 0
HEADS_PER_STEP = 2
ONES_ROWS = 16
GLA_ROWS = 256
TM_ROPE = 1024
TN_ADA = 1536

NEG = -1e30


def _cparams(sem):
    return pltpu.CompilerParams(dimension_semantics=sem, vmem_limit_bytes=VMEM_LIMIT)


def _ada_kernel(c_ref, w_ref, b_ref, o_ref):
    o_ref[...] = jnp.dot(c_ref[...], w_ref[...].astype(BF16),
                         preferred_element_type=F32) + b_ref[...]


def _ada(c_pad, ada_w, ada_b3):
    n = 3 * D_MODEL
    return pl.pallas_call(
        _ada_kernel,
        grid=(DEPTH, n // TN_ADA),
        in_specs=[
            pl.BlockSpec((ADA_ROWS, D_MODEL), lambda l, j: (0, 0)),
            pl.BlockSpec((None, D_MODEL, TN_ADA), lambda l, j: (l, 0, j)),
            pl.BlockSpec((None, 1, TN_ADA), lambda l, j: (l, 0, j)),
        ],
        out_specs=pl.BlockSpec((None, ADA_ROWS, TN_ADA), lambda l, j: (l, 0, j)),
        out_shape=jax.ShapeDtypeStruct((DEPTH, ADA_ROWS, n), F32),
        compiler_params=_cparams(("parallel", "parallel")),
    )(c_pad, ada_w, ada_b3)


def _rope_kernel(pos_ref, invf_ref, cos_ref, sin_ref):
    ang = invf_ref[...] * pos_ref[...].astype(F32)
    c = jnp.cos(ang)
    s = jnp.sin(ang)
    cos_ref[...] = jnp.concatenate([c, c, c, c], axis=0).T
    sin_ref[...] = jnp.concatenate([-s, -s, s, s], axis=0).T


def _rope_tables(pos3, invf):
    nt = SEQ // TM_ROPE
    return pl.pallas_call(
        _rope_kernel,
        grid=(BATCH, nt),
        in_specs=[
            pl.BlockSpec((None, 1, TM_ROPE), lambda b, t: (b, 0, t)),
            pl.BlockSpec((ROPE_HALF, 1), lambda b, t: (0, 0)),
        ],
        out_specs=[
            pl.BlockSpec((TM_ROPE, LANES), lambda b, t: (b * nt + t, 0)),
            pl.BlockSpec((TM_ROPE, LANES), lambda b, t: (b * nt + t, 0)),
        ],
        out_shape=[jax.ShapeDtypeStruct((M_ROWS, LANES), F32)] * 2,
        compiler_params=_cparams(("parallel", "parallel")),
    )(pos3, invf)


def _pipeline_row_chunks(n_chunks, chunk_dot, chunk_store):
    acc = chunk_dot(0)
    for r in range(n_chunks):
        nxt = chunk_dot(r + 1) if r + 1 < n_chunks else None
        chunk_store(r, acc)
        acc = nxt


def _norm_modulate(x, g, scale, shift):
    ms = jnp.mean(x * x, axis=-1, keepdims=True)
    return x * lax.rsqrt(ms + EPS) * (g * (1.0 + scale)) + shift


def _inproj_da_kernel(x_ref, g_ref, scale_ref, shift_ref, w_ref, cos_ref, sin_ref,
                      o_ref, h_scr, *, n_q_tiles, n_rope_tiles, q_scale):
    j = pl.program_id(1)

    @pl.when(j == 0)
    def _():
        for r in range(TM_IN // RC_IN):
            rows = slice(r * RC_IN, (r + 1) * RC_IN)
            h_scr[rows, :] = _norm_modulate(x_ref[rows, :], g_ref[...], scale_ref[...],
                                            shift_ref[...]).astype(BF16)

    is_rope = j < n_rope_tiles
    sc = jnp.where(j < n_q_tiles, q_scale, 1.0).astype(F32)
    cos = jnp.where(is_rope, cos_ref[...] * sc, 1.0)
    sin = jnp.where(is_rope, sin_ref[...] * sc, 0.0)
    n_chunks = TM_IN // RC_IN

    def chunk_dot(r):
        return jnp.dot(h_scr[r * RC_IN:(r + 1) * RC_IN, :], w_ref[...],
                       preferred_element_type=F32)

    def chunk_store(r, acc):
        rows = slice(r * RC_IN, (r + 1) * RC_IN)
        for gidx in range(acc.shape[1] // LANES):
            cols = slice(gidx * LANES, (gidx + 1) * LANES)
            t = acc[:, cols]
            o_ref[rows, cols] = (t * cos[rows, :]
                                 + pltpu.roll(t, LANES // 2, 1) * sin[rows, :]).astype(BF16)

    _pipeline_row_chunks(n_chunks, chunk_dot, chunk_store)


def _inproj_da(x2, norm_g, mod, w_bf, cos, sin, layer):
    n = w_bf.shape[1]
    seg_tiles = D_INNER // TN_IN
    tiles_per_batch = SEQ // TM_IN

    def mod_idx(which):
        return lambda i, j: ((layer * ADA_ROWS + i // tiles_per_batch) * 3 + which, 0, 0)

    kern = functools.partial(_inproj_da_kernel, n_q_tiles=seg_tiles,
                             n_rope_tiles=2 * seg_tiles,
                             q_scale=DA_HEAD_DIM ** -0.5 * math.log2(math.e))
    return pl.pallas_call(
        kern,
        grid=(M_ROWS // TM_IN, n // TN_IN),
        in_specs=[
            pl.BlockSpec((TM_IN, D_MODEL), lambda i, j: (i, 0)),
            pl.BlockSpec((None, 1, D_MODEL), lambda i, j: (layer, 0, 0)),
            pl.BlockSpec((None, 1, D_MODEL), mod_idx(1)),
            pl.BlockSpec((None, 1, D_MODEL), mod_idx(0)),
            pl.BlockSpec((D_MODEL, TN_IN), lambda i, j: (0, j)),
            pl.BlockSpec((TM_IN, LANES), lambda i, j: (i, 0)),
            pl.BlockSpec((TM_IN, LANES), lambda i, j: (i, 0)),
        ],
        out_specs=pl.BlockSpec((None, TM_IN, TN_IN),
                               lambda i, j: (j // seg_tiles, i, j % seg_tiles)),
        out_shape=jax.ShapeDtypeStruct((4, M_ROWS, D_INNER), BF16),
        scratch_shapes=[pltpu.VMEM((TM_IN, D_MODEL), BF16)],
        compiler_params=_cparams(("parallel", "arbitrary")),
    )(x2, norm_g, mod, mod, w_bf, cos, sin)


def _inproj_gla_kernel(x_ref, g_ref, scale_ref, shift_ref, w_ref, wg_ref,
                       o_ref, glow_ref, h_scr):
    j = pl.program_id(1)

    @pl.when(j == 0)
    def _():
        for r in range(TM_IN // RC_IN):
            rows = slice(r * RC_IN, (r + 1) * RC_IN)
            h = _norm_modulate(x_ref[rows, :], g_ref[...], scale_ref[...],
                               shift_ref[...]).astype(BF16)
            h_scr[rows, :] = h
            glow_ref[rows, :] = jnp.dot(h, wg_ref[...], preferred_element_type=F32)

    def chunk_dot(r):
        return jnp.dot(h_scr[r * RC_IN:(r + 1) * RC_IN, :], w_ref[...],
                       preferred_element_type=F32)

    def chunk_store(r, acc):
        o_ref[r * RC_IN:(r + 1) * RC_IN, :] = acc.astype(BF16)

    _pipeline_row_chunks(TM_IN // RC_IN, chunk_dot, chunk_store)


def _inproj_gla(x2, norm_g, mod, w_bf, wg_bf, layer):
    n = GLA_MAIN
    tiles_per_batch = SEQ // TM_IN

    def mod_idx(which):
        return lambda i, j: ((layer * ADA_ROWS + i // tiles_per_batch) * 3 + which, 0, 0)

    return pl.pallas_call(
        _inproj_gla_kernel,
        grid=(M_ROWS // TM_IN, n // TN_IN),
        in_specs=[
            pl.BlockSpec((TM_IN, D_MODEL), lambda i, j: (i, 0)),
            pl.BlockSpec((None, 1, D_MODEL), lambda i, j: (layer, 0, 0)),
            pl.BlockSpec((None, 1, D_MODEL), mod_idx(1)),
            pl.BlockSpec((None, 1, D_MODEL), mod_idx(0)),
            pl.BlockSpec((D_MODEL, TN_IN), lambda i, j: (0, j)),
            pl.BlockSpec((D_MODEL, LANES), lambda i, j: (0, 0)),
        ],
        out_specs=[
            pl.BlockSpec((TM_IN, TN_IN), lambda i, j: (i, j)),
            pl.BlockSpec((TM_IN, LANES), lambda i, j: (i, 0)),
        ],
        out_shape=[jax.ShapeDtypeStruct((M_ROWS, n), BF16),
                   jax.ShapeDtypeStruct((M_ROWS, LANES), F32)],
        scratch_shapes=[pltpu.VMEM((TM_IN, D_MODEL), BF16)],
        compiler_params=_cparams(("parallel", "arbitrary")),
    )(x2, norm_g, mod, mod, w_bf, wg_bf)


def _outproj_kernel(o_ref, w_ref, x_ref, gate_ref, fg_ref, out_ref, *, final_norm):
    def chunk_dot(r):
        return jnp.dot(o_ref[r * RC_OUT:(r + 1) * RC_OUT, :], w_ref[...],
                       preferred_element_type=F32)

    def chunk_store(r, y):
        rows = slice(r * RC_OUT, (r + 1) * RC_OUT)
        x = x_ref[rows, :] + gate_ref[...] * y
        if final_norm:
            ms = jnp.mean(x * x, axis=-1, keepdims=True)
            x = x * lax.rsqrt(ms + EPS) * fg_ref[...]
        out_ref[rows, :] = x

    _pipeline_row_chunks(TM_OUT // RC_OUT, chunk_dot, chunk_store)


def _outproj(o, w_bf, x2, mod, final_g, layer, final_norm):
    tiles_per_batch = SEQ // TM_OUT
    kern = functools.partial(_outproj_kernel, final_norm=final_norm)
    return pl.pallas_call(
        kern,
        grid=(M_ROWS // TM_OUT,),
        in_specs=[
            pl.BlockSpec((TM_OUT, D_INNER), lambda i: (i, 0)),
            pl.BlockSpec((D_INNER, D_MODEL), lambda i: (0, 0)),
            pl.BlockSpec((TM_OUT, D_MODEL), lambda i: (i, 0)),
            pl.BlockSpec((None, 1, D_MODEL),
                         lambda i: ((layer * ADA_ROWS + i // tiles_per_batch) * 3 + 2, 0, 0)),
            pl.BlockSpec((1, D_MODEL), lambda i: (0, 0)),
        ],
        out_specs=pl.BlockSpec((TM_OUT, D_MODEL), lambda i: (i, 0)),
        out_shape=jax.ShapeDtypeStruct((M_ROWS, D_MODEL), F32),
        compiler_params=_cparams(("parallel",)),
    )(o, w_bf, x2, mod, final_g)


def _aligned(start, multiple):
    return start if isinstance(start, int) else pl.multiple_of(start, multiple)


def _head_lanes(hh):
    return slice(hh * LANES, (hh + 1) * LANES)


def _diff_attn_tile(qi, accs, finish_previous, q_ref, k_ref, vt_scr, m_scr, p_scr, alpha_scr):
    heads = range(len(accs))
    lane = lax.broadcasted_iota(jnp.int32, (TQ, LANES), 1)
    is_a = (lane % DA_HEAD_DIM) < ROPE_HALF
    qs = []
    for hh in heads:
        q = q_ref[pl.ds(_aligned(qi * TQ, TQ), TQ), _head_lanes(hh)]
        zero = jnp.zeros_like(q)
        qs.append(jnp.concatenate([jnp.where(is_a, q, zero), jnp.where(is_a, zero, q)],
                                  axis=0))
        accs[hh][...] = jnp.zeros(accs[hh].shape, F32)
    m_scr[...] = jnp.full(m_scr.shape, NEG, F32)

    def deferred_pv(hh, prev_start, cs):
        vt = vt_scr[hh, :, pl.ds(_aligned(prev_start, TK), TK)]
        accs[hh][:, cs] = alpha_scr[hh, :, cs] * accs[hh][:, cs] + jnp.dot(
            vt, p_scr[hh, :, cs], preferred_element_type=F32)

    def block_masked_out(diag_offset, c):
        return diag_offset is not None and diag_offset > (c * CB) % TQ + CB - 1

    def key_round(start, diag_offset, prev_start, prev_diag_offset=None):
        ks = [k_ref[pl.ds(_aligned(start, TK), TK), _head_lanes(hh)] for hh in heads]
        for c in range(2 * TQ // CB):
            cs = slice(c * CB, (c + 1) * CB)
            q_lo = (c * CB) % TQ
            masked_out = block_masked_out(diag_offset, c)
            needs_mask = (diag_offset is not None and not masked_out
                          and diag_offset + TK - 1 > q_lo)
            for hh in heads:
                if not masked_out:
                    st = lax.dot_general(ks[hh], qs[hh][cs], (((1,), (1,)), ((), ())),
                                         preferred_element_type=F32)
                if prev_start is not None and not block_masked_out(prev_diag_offset, c):
                    deferred_pv(hh, prev_start, cs)
                if masked_out:
                    alpha_scr[hh, :, cs] = jnp.ones((1, CB), F32)
                    p_scr[hh, :, cs] = jnp.zeros((TK, CB), BF16)
                    continue
                if needs_mask:
                    kpos = lax.broadcasted_iota(jnp.int32, st.shape, 0) + diag_offset
                    qpos = lax.broadcasted_iota(jnp.int32, st.shape, 1) + q_lo
                    st = jnp.where(kpos <= qpos, st, NEG)
                m_prev = m_scr[hh, :, cs]
                m_new = jnp.maximum(m_prev, jnp.max(st, axis=0, keepdims=True))
                alpha_scr[hh, :, cs] = jnp.exp2(m_prev - m_new)
                p_scr[hh, :, cs] = jnp.exp2(st - m_new).astype(BF16)
                m_scr[hh, :, cs] = m_new

    diag = qi * TQ
    last_sub = (NSUB - 1) * TK
    key_round(diag, 0, None)
    for u in range(1, NSUB):
        key_round(diag + u * TK, u * TK, diag + (u - 1) * TK, (u - 1) * TK)

    def body(kc, carry):
        base = kc * TQ
        key_round(base, None, jnp.where(kc == 0, diag + last_sub, base - TK))
        for u in range(1, NSUB):
            key_round(base + u * TK, None, base + (u - 1) * TK)
        return carry

    finish_previous()
    if not (isinstance(qi, int) and qi == 0):
        lax.fori_loop(0, qi, body, 0)
    last_start = last_sub if isinstance(qi, int) and qi == 0 else diag - TK
    for c in range(2 * TQ // CB):
        for hh in heads:
            deferred_pv(hh, last_start, slice(c * CB, (c + 1) * CB))


def _diff_attn_kernel(q_ref, k_ref, v_ref, z_ref, lamv_ref, subw_ref, o_ref,
                      vt_scr, m_scr, acc_scr, p_scr, alpha_scr, *, lambda_init):
    nq = SEQ // TQ
    heads = range(HEADS_PER_STEP)
    for hh in heads:
        vt_scr[hh, 0:DA_V_DIM, :] = v_ref[:, _head_lanes(hh)].astype(F32).T.astype(BF16)
        vt_scr[hh, DA_V_DIM:, :] = jnp.ones((ONES_ROWS, SEQ), BF16)
    lamv = lamv_ref[...]
    lam = (jnp.exp(jnp.sum(lamv[0:1, :] * lamv[1:2, :], keepdims=True))
           - jnp.exp(jnp.sum(lamv[2:3, :] * lamv[3:4, :], keepdims=True)) + lambda_init)
    subw = subw_ref[...] * (1.0 - lambda_init)

    def epilogue(qi, slot):
        rows = pl.ds(_aligned(qi * TQ, TQ), TQ)
        for hh in heads:
            a = acc_scr[slot, hh]
            on = a[0:DA_V_DIM, :] / a[DA_V_DIM:DA_V_DIM + 1, :]
            ot = on[:, :TQ] - lam * on[:, TQ:]
            ms = jnp.mean(ot * ot, axis=0, keepdims=True)
            ot = ot * lax.rsqrt(ms + EPS) * subw
            z = z_ref[rows, _head_lanes(hh)].astype(F32)
            o_ref[rows, _head_lanes(hh)] = (ot.T * (z * jax.nn.sigmoid(z))).astype(BF16)

    tile = functools.partial(_diff_attn_tile, q_ref=q_ref, k_ref=k_ref, vt_scr=vt_scr,
                             m_scr=m_scr, p_scr=p_scr, alpha_scr=alpha_scr)
    tile(0, [acc_scr.at[0, hh] for hh in heads], lambda: None)

    def tile_body(qi, carry):
        slot = qi % 2
        tile(qi, [acc_scr.at[slot, hh] for hh in heads], lambda: epilogue(qi - 1, 1 - slot))
        return carry

    lax.fori_loop(1, nq, tile_body, 0)
    epilogue(nq - 1, (nq - 1) % 2)


def _diff_attn(qkvz, lamv, subw_col, lambda_init):
    kern = functools.partial(_diff_attn_kernel, lambda_init=lambda_init)
    width = HEADS_PER_STEP * LANES
    head_block = lambda which: pl.BlockSpec((None, SEQ, width), lambda b, h: (which, b, h))
    acc_rows = DA_V_DIM + ONES_ROWS
    return pl.pallas_call(
        kern,
        grid=(BATCH, DA_HEADS // HEADS_PER_STEP),
        in_specs=[
            head_block(0), head_block(1), head_block(2), head_block(3),
            pl.BlockSpec((8, LANES), lambda b, h: (0, 0)),
            pl.BlockSpec((DA_V_DIM, 1), lambda b, h: (0, 0)),
        ],
        out_specs=pl.BlockSpec((SEQ, width), lambda b, h: (b, h)),
        out_shape=jax.ShapeDtypeStruct((M_ROWS, D_INNER), BF16),
        scratch_shapes=[
            pltpu.VMEM((HEADS_PER_STEP, acc_rows, SEQ), BF16),
            pltpu.VMEM((HEADS_PER_STEP, 1, 2 * TQ), F32),
            pltpu.VMEM((2, HEADS_PER_STEP, acc_rows, 2 * TQ), F32),
            pltpu.VMEM((HEADS_PER_STEP, TK, 2 * TQ), BF16),
            pltpu.VMEM((HEADS_PER_STEP, 1, 2 * TQ), F32),
        ],
        compiler_params=_cparams(("parallel", "parallel")),
    )(qkvz, qkvz, qkvz, qkvz, lamv, subw_col)


def _gla_kernel(q_ref, k_ref, v_ref, z_ref, g_ref, wg_ref, bg_ref, tri_ref, nw_ref,
                o_ref, st_scr):
    t = pl.program_id(1)

    @pl.when(t == 0)
    def _():
        st_scr[...] = jnp.zeros(st_scr.shape, F32)

    nc = GLA_ROWS // GLA_CHUNK
    pre = jnp.dot(g_ref[...].astype(BF16), wg_ref[...], preferred_element_type=F32) + bg_ref[...]
    log_a = (jnp.minimum(pre, 0.0) - jnp.log(1.0 + jnp.exp(-jnp.abs(pre)))) * (1.0 / GLA_GATE_TAU)
    hi = log_a.astype(BF16)
    lo = (log_a - hi.astype(F32)).astype(BF16)
    tri = tri_ref[...]
    bcum = (jnp.dot(tri, hi, preferred_element_type=F32)
            + jnp.dot(tri, lo, preferred_element_type=F32))

    row = lax.broadcasted_iota(jnp.int32, (GLA_ROWS, GLA_ROWS), 0)
    col = lax.broadcasted_iota(jnp.int32, (GLA_ROWS, GLA_ROWS), 1)
    causal = col <= row
    nw = nw_ref[...]
    nt_dims = (((1,), (1,)), ((), ()))

    for h in range(GLA_HEADS):
        ks = slice(h * GLA_HEAD_K, (h + 1) * GLA_HEAD_K)
        vs = slice(h * GLA_HEAD_V, (h + 1) * GLA_HEAD_V)
        bh = bcum[:, ks]
        b_end = [bh[(i + 1) * GLA_CHUNK - 1:(i + 1) * GLA_CHUNK, :] for i in range(nc)]
        q_dec, q_int, k_inv, k_end, k_st = [], [], [], [], []
        for i in range(nc):
            rows = slice(i * GLA_CHUNK, (i + 1) * GLA_CHUNK)
            b = bh[rows, :] if i == 0 else bh[rows, :] - b_end[i - 1]
            qd = q_ref[rows, ks].astype(F32) * (GLA_HEAD_K ** -0.5) * jnp.exp(b)
            ki = k_ref[rows, ks].astype(F32) * jnp.exp(-b)
            ke = ki * jnp.exp(b_end[i] if i == 0 else b_end[i] - b_end[i - 1])
            q_dec.append(qd.astype(BF16))
            q_int.append((qd if i == 0 else qd * jnp.exp(b_end[i - 1])).astype(BF16))
            k_inv.append(ki.astype(BF16))
            k_end.append(ke)
            k_st.append((ke if i == nc - 1 else ke * jnp.exp(b_end[nc - 1] - b_end[i]))
                        .astype(BF16))
        attn_rows = []
        for i in range(nc):
            pieces = []
            for j in range(i):
                kj = k_end[j] if j == i - 1 else k_end[j] * jnp.exp(b_end[i - 1] - b_end[j])
                pieces.append(kj.astype(BF16))
            pieces.append(k_inv[i])
            if i < nc - 1:
                pieces.append(jnp.zeros(((nc - 1 - i) * GLA_CHUNK, GLA_HEAD_K), BF16))
            attn_rows.append(lax.dot_general(q_dec[i], jnp.concatenate(pieces, axis=0), nt_dims,
                                             preferred_element_type=F32))
        attn = jnp.where(causal, jnp.concatenate(attn_rows, axis=0), 0.0).astype(BF16)
        st = st_scr[h]
        v = v_ref[:, vs]
        o = jnp.dot(jnp.concatenate([attn] + [jnp.concatenate(q_int, axis=0)], axis=1),
                    jnp.concatenate([v, st.astype(BF16)], axis=0),
                    preferred_element_type=F32)
        dec_col = jnp.broadcast_to(jnp.exp(b_end[nc - 1]), (LANES, GLA_HEAD_K)).T
        dec = jnp.concatenate([dec_col] * (GLA_HEAD_V // LANES), axis=1)
        st_scr[h] = st * dec + lax.dot_general(
            jnp.concatenate(k_st, axis=0), v, (((0,), (0,)), ((), ())),
            preferred_element_type=F32)
        ms = jnp.mean(o * o, axis=-1, keepdims=True)
        z = z_ref[:, vs].astype(F32)
        o_ref[:, vs] = (o * lax.rsqrt(ms + EPS) * nw * (z * jax.nn.sigmoid(z))).astype(BF16)


def _gla(main, glow, wg_bf, bg, tri, nw):
    nt = SEQ // GLA_ROWS
    assert 2 * GLA_DK == D_INNER
    return pl.pallas_call(
        _gla_kernel,
        grid=(BATCH, nt),
        in_specs=[
            pl.BlockSpec((GLA_ROWS, GLA_DK), lambda b, t: (b * nt + t, 0)),
            pl.BlockSpec((GLA_ROWS, GLA_DK), lambda b, t: (b * nt + t, 1)),
            pl.BlockSpec((GLA_ROWS, D_INNER), lambda b, t: (b * nt + t, 1)),
            pl.BlockSpec((GLA_ROWS, D_INNER), lambda b, t: (b * nt + t, 2)),
            pl.BlockSpec((GLA_ROWS, LANES), lambda b, t: (b * nt + t, 0)),
            pl.BlockSpec((LANES, GLA_DK), lambda b, t: (0, 0)),
            pl.BlockSpec((1, GLA_DK), lambda b, t: (0, 0)),
            pl.BlockSpec((GLA_ROWS, GLA_ROWS), lambda b, t: (0, 0)),
            pl.BlockSpec((1, GLA_HEAD_V), lambda b, t: (0, 0)),
        ],
        out_specs=pl.BlockSpec((GLA_ROWS, D_INNER), lambda b, t: (b * nt + t, 0)),
        out_shape=jax.ShapeDtypeStruct((M_ROWS, D_INNER), BF16),
        scratch_shapes=[pltpu.VMEM((GLA_HEADS, GLA_HEAD_K, GLA_HEAD_V), F32)],
        compiler_params=_cparams(("parallel", "arbitrary")),
    )(main, main, main, main, glow, wg_bf, bg, tri, nw)


def _prep_da_w_kernel(w_ref, o_ref, *, n_qk_tiles):
    j = pl.program_id(0)

    @pl.when(j < n_qk_tiles)
    def _():
        lane = lax.broadcasted_iota(jnp.int32, (w_ref.shape[0], LANES), 1)
        take_next = (lane >= ROPE_HALF) & (lane < 2 * ROPE_HALF)
        take_prev = (lane >= 2 * ROPE_HALF) & (lane < 3 * ROPE_HALF)
        for gidx in range(w_ref.shape[1] // LANES):
            cols = slice(gidx * LANES, (gidx + 1) * LANES)
            w = w_ref[:, cols]
            nxt = pltpu.roll(w, LANES - ROPE_HALF, 1)
            prv = pltpu.roll(w, ROPE_HALF, 1)
            o_ref[:, cols] = jnp.where(take_next, nxt,
                                       jnp.where(take_prev, prv, w)).astype(BF16)

    @pl.when(j >= n_qk_tiles)
    def _():
        o_ref[...] = w_ref[...].astype(BF16)


def _prep_da_w(w_in):
    d, n = w_in.shape
    kern = functools.partial(_prep_da_w_kernel, n_qk_tiles=2 * D_INNER // TN_IN)
    return pl.pallas_call(
        kern,
        grid=(n // TN_IN,),
        in_specs=[pl.BlockSpec((d, TN_IN), lambda j: (0, j))],
        out_specs=pl.BlockSpec((d, TN_IN), lambda j: (0, j)),
        out_shape=jax.ShapeDtypeStruct((d, n), BF16),
        compiler_params=_cparams(("parallel",)),
    )(w_in)


def kernel(x, c, positions, ada_w, ada_b, norm_g, da_w_in, da_lam_q1, da_lam_k1, da_lam_q2,
           da_lam_k2, da_subln_w, da_w_out, gla_w_in, gla_w_gate_up, gla_b_gate, gla_norm_w,
           gla_w_out, final_g):
    x2 = x.reshape(M_ROWS, D_MODEL)
    c_pad = jnp.pad(c, ((0, ADA_ROWS - BATCH), (0, 0))).astype(BF16)
    mod = _ada(c_pad, ada_w, ada_b.reshape(DEPTH, 1, 3 * D_MODEL))
    mod = mod.reshape(DEPTH * ADA_ROWS * 3, 1, D_MODEL)
    norm_g3 = norm_g.reshape(DEPTH, 1, D_MODEL)

    inv_freq = ROPE_THETA ** (-jnp.arange(0, DA_HEAD_DIM, 2, dtype=F32) / DA_HEAD_DIM)
    cos, sin = _rope_tables(positions.reshape(BATCH, 1, SEQ), inv_freq.reshape(ROPE_HALF, 1))

    lambda_init = 0.8 - 0.6 * math.exp(-0.3 * 0)
    qkvz = _inproj_da(x2, norm_g3, mod, _prep_da_w(da_w_in[0]), cos, sin, layer=0)
    lamv = jnp.concatenate([da_lam_q1, da_lam_k1, da_lam_q2, da_lam_k2], axis=0)
    lamv = jnp.pad(lamv, ((0, 4), (0, LANES - DA_HEAD_DIM)))
    o = _diff_attn(qkvz, lamv, da_subln_w[0].reshape(DA_V_DIM, 1), lambda_init)
    x2 = _outproj(o, da_w_out[0].astype(BF16), x2, mod, final_g.reshape(1, D_MODEL),
                  layer=0, final_norm=False)

    w = gla_w_in[0]
    wg_in = jnp.pad(w[:, GLA_MAIN:], ((0, 0), (0, LANES - GLA_GATE_RANK))).astype(BF16)
    main, glow = _inproj_gla(x2, norm_g3, mod, w.astype(BF16), wg_in, layer=1)
    wg_up = jnp.pad(gla_w_gate_up[0], ((0, LANES - GLA_GATE_RANK), (0, 0))).astype(BF16)
    r = jnp.arange(GLA_ROWS)
    tri = (r[:, None] >= r[None, :]).astype(BF16)
    o = _gla(main, glow, wg_up, gla_b_gate, tri, gla_norm_w)
    out = _outproj(o, gla_w_out[0].astype(BF16), x2, mod, final_g.reshape(1, D_MODEL),
                   layer=1, final_norm=True)
    return out.reshape(BATCH, SEQ, D_MODEL)
```

```python
import functools
import math

import jax
import jax.numpy as jnp
from jax import lax
from jax.experimental import pallas as pl
from jax.experimental.pallas import tpu as pltpu

F32 = jnp.float32
BF16 = jnp.bfloat16

D_MODEL = 1024
BATCH = 4
SEQ = 4096
DEPTH = 2
D_INNER = 2 * D_MODEL
EPS = 1e-6
M_ROWS = BATCH * SEQ

DA_HEADS = 16
DA_HEAD_DIM = 64
DA_V_DIM = 128
ROPE_THETA = 10000.0
ROPE_HALF = DA_HEAD_DIM // 2

GLA_HEADS = 4
GLA_DK = 1024
GLA_HEAD_K = 256
GLA_HEAD_V = 512
GLA_GATE_RANK = 16
GLA_GATE_TAU = 16.0
GLA_CHUNK = 64
GLA_MAIN = 2 * GLA_DK + 2 * D_INNER

LANES = 128
ADA_ROWS = 16
VMEM_LIMIT = 56 * 1024 * 1024

TM_IN, TN_IN = 2048, 1024
RC_IN = 256
RC_OUT = 128
TM_OUT = 512
TQ = 1024
TK = 256
NSUB = TQ // TK
CB = 256
assert TQ % TK == 0 and TQ % CB == 0
HEADS_PER_STEP = 2
ONES_ROWS = 16
GLA_ROWS = 256
TM_ROPE = 1024
TN_ADA = 1536

NEG = -1e30


def _cparams(sem):
    return pltpu.CompilerParams(dimension_semantics=sem, vmem_limit_bytes=VMEM_LIMIT)


def _ada_kernel(c_ref, w_ref, b_ref, o_ref):
    o_ref[...] = jnp.dot(c_ref[...], w_ref[...].astype(BF16),
                         preferred_element_type=F32) + b_ref[...]


def _ada(c_pad, ada_w, ada_b3):
    n = 3 * D_MODEL
    return pl.pallas_call(
        _ada_kernel,
        grid=(DEPTH, n // TN_ADA),
        in_specs=[
            pl.BlockSpec((ADA_ROWS, D_MODEL), lambda l, j: (0, 0)),
            pl.BlockSpec((None, D_MODEL, TN_ADA), lambda l, j: (l, 0, j)),
            pl.BlockSpec((None, 1, TN_ADA), lambda l, j: (l, 0, j)),
        ],
        out_specs=pl.BlockSpec((None, ADA_ROWS, TN_ADA), lambda l, j: (l, 0, j)),
        out_shape=jax.ShapeDtypeStruct((DEPTH, ADA_ROWS, n), F32),
        compiler_params=_cparams(("parallel", "parallel")),
    )(c_pad, ada_w, ada_b3)


def _rope_kernel(pos_ref, invf_ref, cos_ref, sin_ref):
    ang = invf_ref[...] * pos_ref[...].astype(F32)
    c = jnp.cos(ang)
    s = jnp.sin(ang)
    cos_ref[...] = jnp.concatenate([c, c, c, c], axis=0).T
    sin_ref[...] = jnp.concatenate([-s, -s, s, s], axis=0).T


def _rope_tables(pos3, invf):
    nt = SEQ // TM_ROPE
    return pl.pallas_call(
        _rope_kernel,
        grid=(BATCH, nt),
        in_specs=[
            pl.BlockSpec((None, 1, TM_ROPE), lambda b, t: (b, 0, t)),
            pl.BlockSpec((ROPE_HALF, 1), lambda b, t: (0, 0)),
        ],
        out_specs=[
            pl.BlockSpec((TM_ROPE, LANES), lambda b, t: (b * nt + t, 0)),
            pl.BlockSpec((TM_ROPE, LANES), lambda b, t: (b * nt + t, 0)),
        ],
        out_shape=[jax.ShapeDtypeStruct((M_ROWS, LANES), F32)] * 2,
        compiler_params=_cparams(("parallel", "parallel")),
    )(pos3, invf)


def _pipeline_row_chunks(n_chunks, chunk_dot, chunk_store):
    acc = chunk_dot(0)
    for r in range(n_chunks):
        nxt = chunk_dot(r + 1) if r + 1 < n_chunks else None
        chunk_store(r, acc)
        acc = nxt


def _norm_modulate(x, g, scale, shift):
    ms = jnp.mean(x * x, axis=-1, keepdims=True)
    return x * lax.rsqrt(ms + EPS) * (g * (1.0 + scale)) + shift


def _inproj_da_kernel(x_ref, g_ref, scale_ref, shift_ref, w_ref, cos_ref, sin_ref,
                      o_ref, h_scr, *, n_q_tiles, n_rope_tiles, q_scale):
    j = pl.program_id(1)

    @pl.when(j == 0)
    def _():
        for r in range(TM_IN // RC_IN):
            rows = slice(r * RC_IN, (r + 1) * RC_IN)
            h_scr[rows, :] = _norm_modulate(x_ref[rows, :], g_ref[...], scale_ref[...],
                                            shift_ref[...]).astype(BF16)

    is_rope = j < n_rope_tiles
    sc = jnp.where(j < n_q_tiles, q_scale, 1.0).astype(F32)
    cos = jnp.where(is_rope, cos_ref[...] * sc, 1.0)
    sin = jnp.where(is_rope, sin_ref[...] * sc, 0.0)
    n_chunks = TM_IN // RC_IN

    def chunk_dot(r):
        return jnp.dot(h_scr[r * RC_IN:(r + 1) * RC_IN, :], w_ref[...],
                       preferred_element_type=F32)

    def chunk_store(r, acc):
        rows = slice(r * RC_IN, (r + 1) * RC_IN)
        for gidx in range(acc.shape[1] // LANES):
            cols = slice(gidx * LANES, (gidx + 1) * LANES)
            t = acc[:, cols]
            o_ref[rows, cols] = (t * cos[rows, :]
                                 + pltpu.roll(t, LANES // 2, 1) * sin[rows, :]).astype(BF16)

    _pipeline_row_chunks(n_chunks, chunk_dot, chunk_store)


def _inproj_da(x2, norm_g, mod, w_bf, cos, sin, layer):
    n = w_bf.shape[1]
    seg_tiles = D_INNER // TN_IN
    tiles_per_batch = SEQ // TM_IN

    def mod_idx(which):
        return lambda i, j: ((layer * ADA_ROWS + i // tiles_per_batch) * 3 + which, 0, 0)

    kern = functools.partial(_inproj_da_kernel, n_q_tiles=seg_tiles,
                             n_rope_tiles=2 * seg_tiles,
                             q_scale=DA_HEAD_DIM ** -0.5 * math.log2(math.e))
    return pl.pallas_call(
        kern,
        grid=(M_ROWS // TM_IN, n // TN_IN),
        in_specs=[
            pl.BlockSpec((TM_IN, D_MODEL), lambda i, j: (i, 0)),
            pl.BlockSpec((None, 1, D_MODEL), lambda i, j: (layer, 0, 0)),
            pl.BlockSpec((None, 1, D_MODEL), mod_idx(1)),
            pl.BlockSpec((None, 1, D_MODEL), mod_idx(0)),
            pl.BlockSpec((D_MODEL, TN_IN), lambda i, j: (0, j)),
            pl.BlockSpec((TM_IN, LANES), lambda i, j: (i, 0)),
            pl.BlockSpec((TM_IN, LANES), lambda i, j: (i, 0)),
        ],
        out_specs=pl.BlockSpec((None, TM_IN, TN_IN),
                               lambda i, j: (j // seg_tiles, i, j % seg_tiles)),
        out_shape=jax.ShapeDtypeStruct((4, M_ROWS, D_INNER), BF16),
        scratch_shapes=[pltpu.VMEM((TM_IN, D_MODEL), BF16)],
        compiler_params=_cparams(("parallel", "arbitrary")),
    )(x2, norm_g, mod, mod, w_bf, cos, sin)


def _inproj_gla_kernel(x_ref, g_ref, scale_ref, shift_ref, w_ref, wg_ref,
                       o_ref, glow_ref, h_scr):
    j = pl.program_id(1)

    @pl.when(j == 0)
    def _():
        for r in range(TM_IN // RC_IN):
            rows = slice(r * RC_IN, (r + 1) * RC_IN)
            h = _norm_modulate(x_ref[rows, :], g_ref[...], scale_ref[...],
                               shift_ref[...]).astype(BF16)
            h_scr[rows, :] = h
            glow_ref[rows, :] = jnp.dot(h, wg_ref[...], preferred_element_type=F32)

    def chunk_dot(r):
        return jnp.dot(h_scr[r * RC_IN:(r + 1) * RC_IN, :], w_ref[...],
                       preferred_element_type=F32)

    def chunk_store(r, acc):
        o_ref[r * RC_IN:(r + 1) * RC_IN, :] = acc.astype(BF16)

    _pipeline_row_chunks(TM_IN // RC_IN, chunk_dot, chunk_store)


def _inproj_gla(x2, norm_g, mod, w_bf, wg_bf, layer):
    n = GLA_MAIN
    tiles_per_batch = SEQ // TM_IN

    def mod_idx(which):
        return lambda i, j: ((layer * ADA_ROWS + i // tiles_per_batch) * 3 + which, 0, 0)

    return pl.pallas_call(
        _inproj_gla_kernel,
        grid=(M_ROWS // TM_IN, n // TN_IN),
        in_specs=[
            pl.BlockSpec((TM_IN, D_MODEL), lambda i, j: (i, 0)),
            pl.BlockSpec((None, 1, D_MODEL), lambda i, j: (layer, 0, 0)),
            pl.BlockSpec((None, 1, D_MODEL), mod_idx(1)),
            pl.BlockSpec((None, 1, D_MODEL), mod_idx(0)),
            pl.BlockSpec((D_MODEL, TN_IN), lambda i, j: (0, j)),
            pl.BlockSpec((D_MODEL, LANES), lambda i, j: (0, 0)),
        ],
        out_specs=[
            pl.BlockSpec((TM_IN, TN_IN), lambda i, j: (i, j)),
            pl.BlockSpec((TM_IN, LANES), lambda i, j: (i, 0)),
        ],
        out_shape=[jax.ShapeDtypeStruct((M_ROWS, n), BF16),
                   jax.ShapeDtypeStruct((M_ROWS, LANES), F32)],
        scratch_shapes=[pltpu.VMEM((TM_IN, D_MODEL), BF16)],
        compiler_params=_cparams(("parallel", "arbitrary")),
    )(x2, norm_g, mod, mod, w_bf, wg_bf)


def _outproj_kernel(o_ref, w_ref, x_ref, gate_ref, fg_ref, out_ref, *, final_norm):
    def chunk_dot(r):
        return jnp.dot(o_ref[r * RC_OUT:(r + 1) * RC_OUT, :], w_ref[...],
                       preferred_element_type=F32)

    def chunk_store(r, y):
        rows = slice(r * RC_OUT, (r + 1) * RC_OUT)
        x = x_ref[rows, :] + gate_ref[...] * y
        if final_norm:
            ms = jnp.mean(x * x, axis=-1, keepdims=True)
            x = x * lax.rsqrt(ms + EPS) * fg_ref[...]
        out_ref[rows, :] = x

    _pipeline_row_chunks(TM_OUT // RC_OUT, chunk_dot, chunk_store)


def _outproj(o, w_bf, x2, mod, final_g, layer, final_norm):
    tiles_per_batch = SEQ // TM_OUT
    kern = functools.partial(_outproj_kernel, final_norm=final_norm)
    return pl.pallas_call(
        kern,
        grid=(M_ROWS // TM_OUT,),
        in_specs=[
            pl.BlockSpec((TM_OUT, D_INNER), lambda i: (i, 0)),
            pl.BlockSpec((D_INNER, D_MODEL), lambda i: (0, 0)),
            pl.BlockSpec((TM_OUT, D_MODEL), lambda i: (i, 0)),
            pl.BlockSpec((None, 1, D_MODEL),
                         lambda i: ((layer * ADA_ROWS + i // tiles_per_batch) * 3 + 2, 0, 0)),
            pl.BlockSpec((1, D_MODEL), lambda i: (0, 0)),
        ],
        out_specs=pl.BlockSpec((TM_OUT, D_MODEL), lambda i: (i, 0)),
        out_shape=jax.ShapeDtypeStruct((M_ROWS, D_MODEL), F32),
        compiler_params=_cparams(("parallel",)),
    )(o, w_bf, x2, mod, final_g)


def _aligned(start, multiple):
    return start if isinstance(start, int) else pl.multiple_of(start, multiple)


def _head_lanes(hh):
    return slice(hh * LANES, (hh + 1) * LANES)


def _diff_attn_tile(qi, *, finish_previous, accs, lam, q_ref, k_ref, vt_scr, m_scr, p_scr,
                    alpha_scr, fin_scr):
    heads = range(len(accs))
    lane = lax.broadcasted_iota(jnp.int32, (TQ, LANES), 1)
    is_a = (lane % DA_HEAD_DIM) < ROPE_HALF
    qs = []
    for hh in heads:
        q = q_ref[pl.ds(_aligned(qi * TQ, TQ), TQ), _head_lanes(hh)]
        zero = jnp.zeros_like(q)
        qs.append(jnp.concatenate([jnp.where(is_a, q, zero), jnp.where(is_a, zero, q)],
                                  axis=0))
        accs[hh][...] = jnp.zeros(accs[hh].shape, F32)
    m_scr[...] = jnp.full(m_scr.shape, NEG, F32)

    def deferred_pv(hh, prev_start, cs):
        vt = vt_scr[hh, :, pl.ds(_aligned(prev_start, TK), TK)]
        accs[hh][:, cs] = alpha_scr[hh, :, cs] * accs[hh][:, cs] + jnp.dot(
            vt, p_scr[hh, :, cs], preferred_element_type=F32)

    def block_masked_out(diag_offset, c):
        return diag_offset is not None and diag_offset > (c * CB) % TQ + CB - 1

    def key_round(start, diag_offset, prev_start, prev_diag_offset=None):
        ks = [k_ref[pl.ds(_aligned(start, TK), TK), _head_lanes(hh)] for hh in heads]
        for c in range(2 * TQ // CB):
            cs = slice(c * CB, (c + 1) * CB)
            q_lo = (c * CB) % TQ
            masked_out = block_masked_out(diag_offset, c)
            needs_mask = (diag_offset is not None and not masked_out
                          and diag_offset + TK - 1 > q_lo)
            for hh in heads:
                if not masked_out:
                    st = lax.dot_general(ks[hh], qs[hh][cs], (((1,), (1,)), ((), ())),
                                         preferred_element_type=F32)
                if prev_start is not None and not block_masked_out(prev_diag_offset, c):
                    deferred_pv(hh, prev_start, cs)
                if masked_out:
                    alpha_scr[hh, :, cs] = jnp.ones((1, CB), F32)
                    p_scr[hh, :, cs] = jnp.zeros((TK, CB), BF16)
                    continue
                if needs_mask:
                    kpos = lax.broadcasted_iota(jnp.int32, st.shape, 0) + diag_offset
                    qpos = lax.broadcasted_iota(jnp.int32, st.shape, 1) + q_lo
                    st = jnp.where(kpos <= qpos, st, NEG)
                sb = st.astype(BF16)
                m_prev = m_scr[hh, :, cs]
                m_new = jnp.maximum(m_prev, jnp.max(sb, axis=0, keepdims=True).astype(F32))
                alpha_scr[hh, :, cs] = jnp.exp2(m_prev - m_new)
                p_scr[hh, :, cs] = jnp.exp2(sb - m_new.astype(BF16))
                m_scr[hh, :, cs] = m_new

    diag = qi * TQ
    last_sub = (NSUB - 1) * TK
    key_round(diag, 0, None)
    for u in range(1, NSUB):
        key_round(diag + u * TK, u * TK, diag + (u - 1) * TK, (u - 1) * TK)

    def body(kc, carry):
        base = kc * TQ
        key_round(base, None, jnp.where(kc == 0, diag + last_sub, base - TK))
        for u in range(1, NSUB):
            key_round(base + u * TK, None, base + (u - 1) * TK)
        return carry

    finish_previous()
    if not (isinstance(qi, int) and qi == 0):
        lax.fori_loop(0, qi, body, 0)
    last_start = last_sub if isinstance(qi, int) and qi == 0 else diag - TK
    for c in range(2 * TQ // CB):
        for hh in heads:
            deferred_pv(hh, last_start, slice(c * CB, (c + 1) * CB))
    for hh in heads:
        a = accs[hh][...]
        on = a[0:DA_V_DIM, :] / a[DA_V_DIM:DA_V_DIM + 1, :]
        fin_scr[hh] = on[:, :TQ] - lam * on[:, TQ:]


def _diff_attn_kernel(q_ref, k_ref, v_ref, z_ref, lamv_ref, subw_ref, o_ref,
                      vt_scr, m_scr, acc_scr, p_scr, alpha_scr, fin_scr, *, lambda_init):
    nq = SEQ // TQ
    heads = range(HEADS_PER_STEP)
    for hh in heads:
        vt_scr[hh, 0:DA_V_DIM, :] = v_ref[:, _head_lanes(hh)].astype(F32).T.astype(BF16)
        vt_scr[hh, DA_V_DIM:, :] = jnp.ones((ONES_ROWS, SEQ), BF16)
    lamv = lamv_ref[...]
    lam = (jnp.exp(jnp.sum(lamv[0:1, :] * lamv[1:2, :], keepdims=True))
           - jnp.exp(jnp.sum(lamv[2:3, :] * lamv[3:4, :], keepdims=True)) + lambda_init)
    subw = subw_ref[...] * (1.0 - lambda_init)

    def epilogue(qi):
        rows = pl.ds(_aligned(qi * TQ, TQ), TQ)
        for hh in heads:
            ot = fin_scr[hh]
            ms = jnp.mean(ot * ot, axis=0, keepdims=True)
            ot = ot * lax.rsqrt(ms + EPS) * subw
            z = z_ref[rows, _head_lanes(hh)].astype(F32)
            o_ref[rows, _head_lanes(hh)] = (ot.T * (z * jax.nn.sigmoid(z))).astype(BF16)

    tile = functools.partial(_diff_attn_tile, accs=[acc_scr.at[hh] for hh in heads], lam=lam,
                             q_ref=q_ref, k_ref=k_ref, vt_scr=vt_scr, m_scr=m_scr,
                             p_scr=p_scr, alpha_scr=alpha_scr, fin_scr=fin_scr)
    tile(0, finish_previous=lambda: None)

    def tile_body(qi, carry):
        tile(qi, finish_previous=lambda: epilogue(qi - 1))
        return carry

    lax.fori_loop(1, nq, tile_body, 0)
    epilogue(nq - 1)


def _diff_attn(qkvz, lamv, subw_col, lambda_init):
    kern = functools.partial(_diff_attn_kernel, lambda_init=lambda_init)
    width = HEADS_PER_STEP * LANES
    head_block = lambda which: pl.BlockSpec((None, SEQ, width), lambda b, h: (which, b, h))
    acc_rows = DA_V_DIM + ONES_ROWS
    return pl.pallas_call(
        kern,
        grid=(BATCH, DA_HEADS // HEADS_PER_STEP),
        in_specs=[
            head_block(0), head_block(1), head_block(2), head_block(3),
            pl.BlockSpec((8, LANES), lambda b, h: (0, 0)),
            pl.BlockSpec((DA_V_DIM, 1), lambda b, h: (0, 0)),
        ],
        out_specs=pl.BlockSpec((SEQ, width), lambda b, h: (b, h)),
        out_shape=jax.ShapeDtypeStruct((M_ROWS, D_INNER), BF16),
        scratch_shapes=[
            pltpu.VMEM((HEADS_PER_STEP, acc_rows, SEQ), BF16),
            pltpu.VMEM((HEADS_PER_STEP, 1, 2 * TQ), F32),
            pltpu.VMEM((HEADS_PER_STEP, acc_rows, 2 * TQ), F32),
            pltpu.VMEM((HEADS_PER_STEP, TK, 2 * TQ), BF16),
            pltpu.VMEM((HEADS_PER_STEP, 1, 2 * TQ), F32),
            pltpu.VMEM((HEADS_PER_STEP, DA_V_DIM, TQ), F32),
        ],
        compiler_params=_cparams(("parallel", "parallel")),
    )(qkvz, qkvz, qkvz, qkvz, lamv, subw_col)


def _gla_kernel(q_ref, k_ref, v_ref, z_ref, g_ref, wg_ref, bg_ref, tri_ref, nw_ref,
                o_ref, st_scr):
    t = pl.program_id(1)

    @pl.when(t == 0)
    def _():
        st_scr[...] = jnp.zeros(st_scr.shape, F32)

    nc = GLA_ROWS // GLA_CHUNK
    pre = jnp.dot(g_ref[...].astype(BF16), wg_ref[...], preferred_element_type=F32) + bg_ref[...]
    log_a = (jnp.minimum(pre, 0.0) - jnp.log(1.0 + jnp.exp(-jnp.abs(pre)))) * (1.0 / GLA_GATE_TAU)
    hi = log_a.astype(BF16)
    lo = (log_a - hi.astype(F32)).astype(BF16)
    tri = tri_ref[...]
    bcum = (jnp.dot(tri, hi, preferred_element_type=F32)
            + jnp.dot(tri, lo, preferred_element_type=F32))

    row = lax.broadcasted_iota(jnp.int32, (GLA_ROWS, GLA_ROWS), 0)
    col = lax.broadcasted_iota(jnp.int32, (GLA_ROWS, GLA_ROWS), 1)
    causal = col <= row
    nw = nw_ref[...]
    nt_dims = (((1,), (1,)), ((), ()))

    for h in range(GLA_HEADS):
        ks = slice(h * GLA_HEAD_K, (h + 1) * GLA_HEAD_K)
        vs = slice(h * GLA_HEAD_V, (h + 1) * GLA_HEAD_V)
        bh = bcum[:, ks]
        b_end = [bh[(i + 1) * GLA_CHUNK - 1:(i + 1) * GLA_CHUNK, :] for i in range(nc)]
        q_dec, q_int, k_inv, k_end, k_st = [], [], [], [], []
        for i in range(nc):
            rows = slice(i * GLA_CHUNK, (i + 1) * GLA_CHUNK)
            b = bh[rows, :] if i == 0 else bh[rows, :] - b_end[i - 1]
            qd = q_ref[rows, ks].astype(F32) * (GLA_HEAD_K ** -0.5) * jnp.exp(b)
            ki = k_ref[rows, ks].astype(F32) * jnp.exp(-b)
            ke = ki * jnp.exp(b_end[i] if i == 0 else b_end[i] - b_end[i - 1])
            q_dec.append(qd.astype(BF16))
            q_int.append((qd if i == 0 else qd * jnp.exp(b_end[i - 1])).astype(BF16))
            k_inv.append(ki.astype(BF16))
            k_end.append(ke)
            k_st.append((ke if i == nc - 1 else ke * jnp.exp(b_end[nc - 1] - b_end[i]))
                        .astype(BF16))
        attn_rows = []
        for i in range(nc):
            pieces = []
            for j in range(i):
                kj = k_end[j] if j == i - 1 else k_end[j] * jnp.exp(b_end[i - 1] - b_end[j])
                pieces.append(kj.astype(BF16))
            pieces.append(k_inv[i])
            if i < nc - 1:
                pieces.append(jnp.zeros(((nc - 1 - i) * GLA_CHUNK, GLA_HEAD_K), BF16))
            attn_rows.append(lax.dot_general(q_dec[i], jnp.concatenate(pieces, axis=0), nt_dims,
                                             preferred_element_type=F32))
        attn = jnp.where(causal, jnp.concatenate(attn_rows, axis=0), 0.0).astype(BF16)
        st = st_scr[h]
        v = v_ref[:, vs]
        o = jnp.dot(jnp.concatenate([attn] + [jnp.concatenate(q_int, axis=0)], axis=1),
                    jnp.concatenate([v, st.astype(BF16)], axis=0),
                    preferred_element_type=F32)
        dec_col = jnp.broadcast_to(jnp.exp(b_end[nc - 1]), (LANES, GLA_HEAD_K)).T
        dec = jnp.concatenate([dec_col] * (GLA_HEAD_V // LANES), axis=1)
        st_scr[h] = st * dec + lax.dot_general(
            jnp.concatenate(k_st, axis=0), v, (((0,), (0,)), ((), ())),
            preferred_element_type=F32)
        ms = jnp.mean(o * o, axis=-1, keepdims=True)
        z = z_ref[:, vs].astype(F32)
        o_ref[:, vs] = (o * lax.rsqrt(ms + EPS) * nw * (z * jax.nn.sigmoid(z))).astype(BF16)


def _gla(main, glow, wg_bf, bg, tri, nw):
    nt = SEQ // GLA_ROWS
    assert 2 * GLA_DK == D_INNER
    return pl.pallas_call(
        _gla_kernel,
        grid=(BATCH, nt),
        in_specs=[
            pl.BlockSpec((GLA_ROWS, GLA_DK), lambda b, t: (b * nt + t, 0)),
            pl.BlockSpec((GLA_ROWS, GLA_DK), lambda b, t: (b * nt + t, 1)),
            pl.BlockSpec((GLA_ROWS, D_INNER), lambda b, t: (b * nt + t, 1)),
            pl.BlockSpec((GLA_ROWS, D_INNER), lambda b, t: (b * nt + t, 2)),
            pl.BlockSpec((GLA_ROWS, LANES), lambda b, t: (b * nt + t, 0)),
            pl.BlockSpec((LANES, GLA_DK), lambda b, t: (0, 0)),
            pl.BlockSpec((1, GLA_DK), lambda b, t: (0, 0)),
            pl.BlockSpec((GLA_ROWS, GLA_ROWS), lambda b, t: (0, 0)),
            pl.BlockSpec((1, GLA_HEAD_V), lambda b, t: (0, 0)),
        ],
        out_specs=pl.BlockSpec((GLA_ROWS, D_INNER), lambda b, t: (b * nt + t, 0)),
        out_shape=jax.ShapeDtypeStruct((M_ROWS, D_INNER), BF16),
        scratch_shapes=[pltpu.VMEM((GLA_HEADS, GLA_HEAD_K, GLA_HEAD_V), F32)],
        compiler_params=_cparams(("parallel", "arbitrary")),
    )(main, main, main, main, glow, wg_bf, bg, tri, nw)


def _prep_da_w_kernel(w_ref, o_ref, *, n_qk_tiles):
    j = pl.program_id(0)

    @pl.when(j < n_qk_tiles)
    def _():
        lane = lax.broadcasted_iota(jnp.int32, (w_ref.shape[0], LANES), 1)
        take_next = (lane >= ROPE_HALF) & (lane < 2 * ROPE_HALF)
        take_prev = (lane >= 2 * ROPE_HALF) & (lane < 3 * ROPE_HALF)
        for gidx in range(w_ref.shape[1] // LANES):
            cols = slice(gidx * LANES, (gidx + 1) * LANES)
            w = w_ref[:, cols]
            nxt = pltpu.roll(w, LANES - ROPE_HALF, 1)
            prv = pltpu.roll(w, ROPE_HALF, 1)
            o_ref[:, cols] = jnp.where(take_next, nxt,
                                       jnp.where(take_prev, prv, w)).astype(BF16)

    @pl.when(j >= n_qk_tiles)
    def _():
        o_ref[...] = w_ref[...].astype(BF16)


def _prep_da_w(w_in):
    d, n = w_in.shape
    kern = functools.partial(_prep_da_w_kernel, n_qk_tiles=2 * D_INNER // TN_IN)
    return pl.pallas_call(
        kern,
        grid=(n // TN_IN,),
        in_specs=[pl.BlockSpec((d, TN_IN), lambda j: (0, j))],
        out_specs=pl.BlockSpec((d, TN_IN), lambda j: (0, j)),
        out_shape=jax.ShapeDtypeStruct((d, n), BF16),
        compiler_params=_cparams(("parallel",)),
    )(w_in)


def kernel(x, c, positions, ada_w, ada_b, norm_g, da_w_in, da_lam_q1, da_lam_k1, da_lam_q2,
           da_lam_k2, da_subln_w, da_w_out, gla_w_in, gla_w_gate_up, gla_b_gate, gla_norm_w,
           gla_w_out, final_g):
    x2 = x.reshape(M_ROWS, D_MODEL)
    c_pad = jnp.pad(c, ((0, ADA_ROWS - BATCH), (0, 0))).astype(BF16)
    mod = _ada(c_pad, ada_w, ada_b.reshape(DEPTH, 1, 3 * D_MODEL))
    mod = mod.reshape(DEPTH * ADA_ROWS * 3, 1, D_MODEL)
    norm_g3 = norm_g.reshape(DEPTH, 1, D_MODEL)

    inv_freq = ROPE_THETA ** (-jnp.arange(0, DA_HEAD_DIM, 2, dtype=F32) / DA_HEAD_DIM)
    cos, sin = _rope_tables(positions.reshape(BATCH, 1, SEQ), inv_freq.reshape(ROPE_HALF, 1))

    lambda_init = 0.8 - 0.6 * math.exp(-0.3 * 0)
    qkvz = _inproj_da(x2, norm_g3, mod, _prep_da_w(da_w_in[0]), cos, sin, layer=0)
    lamv = jnp.concatenate([da_lam_q1, da_lam_k1, da_lam_q2, da_lam_k2], axis=0)
    lamv = jnp.pad(lamv, ((0, 4), (0, LANES - DA_HEAD_DIM)))
    o = _diff_attn(qkvz, lamv, da_subln_w[0].reshape(DA_V_DIM, 1), lambda_init)
    x2 = _outproj(o, da_w_out[0].astype(BF16), x2, mod, final_g.reshape(1, D_MODEL),
                  layer=0, final_norm=False)

    w = gla_w_in[0]
    wg_in = jnp.pad(w[:, GLA_MAIN:], ((0, 0), (0, LANES - GLA_GATE_RANK))).astype(BF16)
    main, glow = _inproj_gla(x2, norm_g3, mod, w.astype(BF16), wg_in, layer=1)
    wg_up = jnp.pad(gla_w_gate_up[0], ((0, LANES - GLA_GATE_RANK), (0, 0))).astype(BF16)
    r = jnp.arange(GLA_ROWS)
    tri = (r[:, None] >= r[None, :]).astype(BF16)
    o = _gla(main, glow, wg_up, gla_b_gate, tri, gla_norm_w)
    out = _outproj(o, gla_w_out[0].astype(BF16), x2, mod, final_g.reshape(1, D_MODEL),
                   layer=1, final_norm=True)
    return out.reshape(BATCH, SEQ, D_MODEL)
```

```python
import functools
import math

import jax
import jax.numpy as jnp
from jax import lax
from jax.experimental import pallas as pl
from jax.experimental.pallas import tpu as pltpu

F32 = jnp.float32
BF16 = jnp.bfloat16

D_MODEL = 1024
BATCH = 4
SEQ = 4096
DEPTH = 2
D_INNER = 2 * D_MODEL
EPS = 1e-6
M_ROWS = BATCH * SEQ

DA_HEADS = 16
DA_HEAD_DIM = 64
DA_V_DIM = 128
ROPE_THETA = 10000.0
ROPE_HALF = DA_HEAD_DIM // 2

GLA_HEADS = 4
GLA_DK = 1024
GLA_HEAD_K = 256
GLA_HEAD_V = 512
GLA_GATE_RANK = 16
GLA_GATE_TAU = 16.0
GLA_CHUNK = 64
GLA_MAIN = 2 * GLA_DK + 2 * D_INNER

LANES = 128
ADA_ROWS = 16
VMEM_LIMIT = 56 * 1024 * 1024

TM_IN, TN_IN = 2048, 1024
RC_IN = 256
RC_OUT = 256
TM_OUT = 1024
TQ = 1024
TK = 256
NSUB = TQ // TK
CB = 256
assert TQ % TK == 0 and TQ % CB == 0
HEADS_PER_STEP = 2
ONES_ROWS = 16
GLA_ROWS = 256
TM_ROPE = 1024
TN_ADA = 1536

NEG = -1e30


def _cparams(sem):
    return pltpu.CompilerParams(dimension_semantics=sem, vmem_limit_bytes=VMEM_LIMIT)


def _ada_kernel(c_ref, w_ref, b_ref, o_ref):
    o_ref[...] = jnp.dot(c_ref[...], w_ref[...].astype(BF16),
                         preferred_element_type=F32) + b_ref[...]


def _ada(c_pad, ada_w, ada_b3):
    n = 3 * D_MODEL
    return pl.pallas_call(
        _ada_kernel,
        grid=(DEPTH, n // TN_ADA),
        in_specs=[
            pl.BlockSpec((ADA_ROWS, D_MODEL), lambda l, j: (0, 0)),
            pl.BlockSpec((None, D_MODEL, TN_ADA), lambda l, j: (l, 0, j)),
            pl.BlockSpec((None, 1, TN_ADA), lambda l, j: (l, 0, j)),
        ],
        out_specs=pl.BlockSpec((None, ADA_ROWS, TN_ADA), lambda l, j: (l, 0, j)),
        out_shape=jax.ShapeDtypeStruct((DEPTH, ADA_ROWS, n), F32),
        compiler_params=_cparams(("parallel", "parallel")),
    )(c_pad, ada_w, ada_b3)


def _rope_kernel(pos_ref, invf_ref, cos_ref, sin_ref):
    ang = invf_ref[...] * pos_ref[...].astype(F32)
    c = jnp.cos(ang)
    s = jnp.sin(ang)
    cos_ref[...] = jnp.concatenate([c, c, c, c], axis=0).T
    sin_ref[...] = jnp.concatenate([-s, -s, s, s], axis=0).T


def _rope_tables(pos3, invf):
    nt = SEQ // TM_ROPE
    return pl.pallas_call(
        _rope_kernel,
        grid=(BATCH, nt),
        in_specs=[
            pl.BlockSpec((None, 1, TM_ROPE), lambda b, t: (b, 0, t)),
            pl.BlockSpec((ROPE_HALF, 1), lambda b, t: (0, 0)),
        ],
        out_specs=[
            pl.BlockSpec((TM_ROPE, LANES), lambda b, t: (b * nt + t, 0)),
            pl.BlockSpec((TM_ROPE, LANES), lambda b, t: (b * nt + t, 0)),
        ],
        out_shape=[jax.ShapeDtypeStruct((M_ROWS, LANES), F32)] * 2,
        compiler_params=_cparams(("parallel", "parallel")),
    )(pos3, invf)


def _pipeline_row_chunks(n_chunks, chunk_dot, chunk_store):
    acc = chunk_dot(0)
    for r in range(n_chunks):
        nxt = chunk_dot(r + 1) if r + 1 < n_chunks else None
        chunk_store(r, acc)
        acc = nxt


def _norm_modulate(x, g, scale, shift):
    ms = jnp.mean(x * x, axis=-1, keepdims=True)
    return x * lax.rsqrt(ms + EPS) * (g * (1.0 + scale)) + shift


def _inproj_da_kernel(x_ref, g_ref, scale_ref, shift_ref, w_ref, cos_ref, sin_ref,
                      o_ref, h_scr, *, n_q_tiles, n_rope_tiles, q_scale):
    j = pl.program_id(1)

    @pl.when(j == 0)
    def _():
        for r in range(TM_IN // RC_IN):
            rows = slice(r * RC_IN, (r + 1) * RC_IN)
            h_scr[rows, :] = _norm_modulate(x_ref[rows, :], g_ref[...], scale_ref[...],
                                            shift_ref[...]).astype(BF16)

    is_rope = j < n_rope_tiles
    sc = jnp.where(j < n_q_tiles, q_scale, 1.0).astype(F32)
    cos = jnp.where(is_rope, cos_ref[...] * sc, 1.0)
    sin = jnp.where(is_rope, sin_ref[...] * sc, 0.0)
    n_chunks = TM_IN // RC_IN

    def chunk_dot(r):
        return jnp.dot(h_scr[r * RC_IN:(r + 1) * RC_IN, :], w_ref[...],
                       preferred_element_type=F32)

    def chunk_store(r, acc):
        rows = slice(r * RC_IN, (r + 1) * RC_IN)
        for gidx in range(acc.shape[1] // LANES):
            cols = slice(gidx * LANES, (gidx + 1) * LANES)
            t = acc[:, cols]
            o_ref[rows, cols] = (t * cos[rows, :]
                                 + pltpu.roll(t, LANES // 2, 1) * sin[rows, :]).astype(BF16)

    _pipeline_row_chunks(n_chunks, chunk_dot, chunk_store)


def _inproj_da(x2, norm_g, mod, w_bf, cos, sin, layer):
    n = w_bf.shape[1]
    seg_tiles = D_INNER // TN_IN
    tiles_per_batch = SEQ // TM_IN

    def mod_idx(which):
        return lambda i, j: ((layer * ADA_ROWS + i // tiles_per_batch) * 3 + which, 0, 0)

    kern = functools.partial(_inproj_da_kernel, n_q_tiles=seg_tiles,
                             n_rope_tiles=2 * seg_tiles,
                             q_scale=DA_HEAD_DIM ** -0.5 * math.log2(math.e))
    return pl.pallas_call(
        kern,
        grid=(M_ROWS // TM_IN, n // TN_IN),
        in_specs=[
            pl.BlockSpec((TM_IN, D_MODEL), lambda i, j: (i, 0)),
            pl.BlockSpec((None, 1, D_MODEL), lambda i, j: (layer, 0, 0)),
            pl.BlockSpec((None, 1, D_MODEL), mod_idx(1)),
            pl.BlockSpec((None, 1, D_MODEL), mod_idx(0)),
            pl.BlockSpec((D_MODEL, TN_IN), lambda i, j: (0, j)),
            pl.BlockSpec((TM_IN, LANES), lambda i, j: (i, 0)),
            pl.BlockSpec((TM_IN, LANES), lambda i, j: (i, 0)),
        ],
        out_specs=pl.BlockSpec((None, TM_IN, TN_IN),
                               lambda i, j: (j // seg_tiles, i, j % seg_tiles)),
        out_shape=jax.ShapeDtypeStruct((4, M_ROWS, D_INNER), BF16),
        scratch_shapes=[pltpu.VMEM((TM_IN, D_MODEL), BF16)],
        compiler_params=_cparams(("parallel", "arbitrary")),
    )(x2, norm_g, mod, mod, w_bf, cos, sin)


def _inproj_gla_kernel(x_ref, g_ref, scale_ref, shift_ref, w_ref, wg_ref,
                       o_ref, glow_ref, h_scr):
    j = pl.program_id(1)

    @pl.when(j == 0)
    def _():
        for r in range(TM_IN // RC_IN):
            rows = slice(r * RC_IN, (r + 1) * RC_IN)
            h = _norm_modulate(x_ref[rows, :], g_ref[...], scale_ref[...],
                               shift_ref[...]).astype(BF16)
            h_scr[rows, :] = h
            glow_ref[rows, :] = jnp.dot(h, wg_ref[...], preferred_element_type=F32)

    def chunk_dot(r):
        return jnp.dot(h_scr[r * RC_IN:(r + 1) * RC_IN, :], w_ref[...],
                       preferred_element_type=F32)

    def chunk_store(r, acc):
        o_ref[r * RC_IN:(r + 1) * RC_IN, :] = acc.astype(BF16)

    _pipeline_row_chunks(TM_IN // RC_IN, chunk_dot, chunk_store)


def _inproj_gla(x2, norm_g, mod, w_bf, wg_bf, layer):
    n = GLA_MAIN
    tiles_per_batch = SEQ // TM_IN

    def mod_idx(which):
        return lambda i, j: ((layer * ADA_ROWS + i // tiles_per_batch) * 3 + which, 0, 0)

    return pl.pallas_call(
        _inproj_gla_kernel,
        grid=(M_ROWS // TM_IN, n // TN_IN),
        in_specs=[
            pl.BlockSpec((TM_IN, D_MODEL), lambda i, j: (i, 0)),
            pl.BlockSpec((None, 1, D_MODEL), lambda i, j: (layer, 0, 0)),
            pl.BlockSpec((None, 1, D_MODEL), mod_idx(1)),
            pl.BlockSpec((None, 1, D_MODEL), mod_idx(0)),
            pl.BlockSpec((D_MODEL, TN_IN), lambda i, j: (0, j)),
            pl.BlockSpec((D_MODEL, LANES), lambda i, j: (0, 0)),
        ],
        out_specs=[
            pl.BlockSpec((TM_IN, TN_IN), lambda i, j: (i, j)),
            pl.BlockSpec((TM_IN, LANES), lambda i, j: (i, 0)),
        ],
        out_shape=[jax.ShapeDtypeStruct((M_ROWS, n), BF16),
                   jax.ShapeDtypeStruct((M_ROWS, LANES), F32)],
        scratch_shapes=[pltpu.VMEM((TM_IN, D_MODEL), BF16)],
        compiler_params=_cparams(("parallel", "arbitrary")),
    )(x2, norm_g, mod, mod, w_bf, wg_bf)


def _outproj_kernel(o_ref, w_ref, x_ref, gate_ref, fg_ref, out_ref, *, final_norm):
    def chunk_dot(r):
        return jnp.dot(o_ref[r * RC_OUT:(r + 1) * RC_OUT, :], w_ref[...],
                       preferred_element_type=F32)

    def chunk_store(r, y):
        rows = slice(r * RC_OUT, (r + 1) * RC_OUT)
        x = x_ref[rows, :] + gate_ref[...] * y
        if final_norm:
            ms = jnp.mean(x * x, axis=-1, keepdims=True)
            x = x * lax.rsqrt(ms + EPS) * fg_ref[...]
        out_ref[rows, :] = x

    _pipeline_row_chunks(TM_OUT // RC_OUT, chunk_dot, chunk_store)


def _outproj(o, w_bf, x2, mod, final_g, layer, final_norm):
    tiles_per_batch = SEQ // TM_OUT
    kern = functools.partial(_outproj_kernel, final_norm=final_norm)
    return pl.pallas_call(
        kern,
        grid=(M_ROWS // TM_OUT,),
        in_specs=[
            pl.BlockSpec((TM_OUT, D_INNER), lambda i: (i, 0)),
            pl.BlockSpec((D_INNER, D_MODEL), lambda i: (0, 0)),
            pl.BlockSpec((TM_OUT, D_MODEL), lambda i: (i, 0)),
            pl.BlockSpec((None, 1, D_MODEL),
                         lambda i: ((layer * ADA_ROWS + i // tiles_per_batch) * 3 + 2, 0, 0)),
            pl.BlockSpec((1, D_MODEL), lambda i: (0, 0)),
        ],
        out_specs=pl.BlockSpec((TM_OUT, D_MODEL), lambda i: (i, 0)),
        out_shape=jax.ShapeDtypeStruct((M_ROWS, D_MODEL), F32),
        compiler_params=_cparams(("parallel",)),
    )(o, w_bf, x2, mod, final_g)


def _aligned(start, multiple):
    return start if isinstance(start, int) else pl.multiple_of(start, multiple)


def _head_lanes(hh):
    return slice(hh * LANES, (hh + 1) * LANES)


def _diff_attn_tile(qi, *, finish_previous, accs, lam, q_ref, k_ref, vt_scr, m_scr, p_scr,
                    alpha_scr, fin_scr):
    heads = range(len(accs))
    lane = lax.broadcasted_iota(jnp.int32, (TQ, LANES), 1)
    is_a = (lane % DA_HEAD_DIM) < ROPE_HALF
    qs = []
    for hh in heads:
        q = q_ref[pl.ds(_aligned(qi * TQ, TQ), TQ), _head_lanes(hh)]
        zero = jnp.zeros_like(q)
        qs.append(jnp.concatenate([jnp.where(is_a, q, zero), jnp.where(is_a, zero, q)],
                                  axis=0))
        accs[hh][...] = jnp.zeros(accs[hh].shape, F32)
    m_scr[...] = jnp.full(m_scr.shape, NEG, F32)

    def deferred_pv(hh, prev_start, cs):
        vt = vt_scr[hh, :, pl.ds(_aligned(prev_start, TK), TK)]
        accs[hh][:, cs] = alpha_scr[hh, :, cs] * accs[hh][:, cs] + jnp.dot(
            vt, p_scr[hh, :, cs], preferred_element_type=F32)

    def block_masked_out(diag_offset, c):
        return diag_offset is not None and diag_offset > (c * CB) % TQ + CB - 1

    def key_round(start, diag_offset, prev_start, prev_diag_offset=None):
        ks = [k_ref[pl.ds(_aligned(start, TK), TK), _head_lanes(hh)] for hh in heads]
        for c in range(2 * TQ // CB):
            cs = slice(c * CB, (c + 1) * CB)
            q_lo = (c * CB) % TQ
            masked_out = block_masked_out(diag_offset, c)
            needs_mask = (diag_offset is not None and not masked_out
                          and diag_offset + TK - 1 > q_lo)
            for hh in heads:
                if not masked_out:
                    st = lax.dot_general(ks[hh], qs[hh][cs], (((1,), (1,)), ((), ())),
                                         preferred_element_type=F32)
                if prev_start is not None and not block_masked_out(prev_diag_offset, c):
                    deferred_pv(hh, prev_start, cs)
                if masked_out:
                    alpha_scr[hh, :, cs] = jnp.ones((1, CB), F32)
                    p_scr[hh, :, cs] = jnp.zeros((TK, CB), BF16)
                    continue
                if needs_mask:
                    kpos = lax.broadcasted_iota(jnp.int32, st.shape, 0) + diag_offset
                    qpos = lax.broadcasted_iota(jnp.int32, st.shape, 1) + q_lo
                    st = jnp.where(kpos <= qpos, st, NEG)
                sb = st.astype(BF16)
                m_prev = m_scr[hh, :, cs]
                m_new = jnp.maximum(m_prev, jnp.max(sb, axis=0, keepdims=True).astype(F32))
                alpha_scr[hh, :, cs] = jnp.exp2(m_prev - m_new)
                p_scr[hh, :, cs] = jnp.exp2(sb - m_new.astype(BF16))
                m_scr[hh, :, cs] = m_new

    diag = qi * TQ
    order = list(range(NSUB - 1, -1, -1))
    key_round(diag + order[0] * TK, order[0] * TK, None)
    for prev, u in zip(order[:-1], order[1:]):
        key_round(diag + u * TK, u * TK, diag + prev * TK, prev * TK)
    diag_last = diag + order[-1] * TK

    def body(kc, carry):
        base = kc * TQ
        key_round(base, None, jnp.where(kc == 0, diag_last, base - TK))
        for u in range(1, NSUB):
            key_round(base + u * TK, None, base + (u - 1) * TK)
        return carry

    finish_previous()
    if not (isinstance(qi, int) and qi == 0):
        lax.fori_loop(0, qi, body, 0)
    last_start = diag_last if isinstance(qi, int) and qi == 0 else diag - TK
    for c in range(2 * TQ // CB):
        for hh in heads:
            deferred_pv(hh, last_start, slice(c * CB, (c + 1) * CB))
    for hh in heads:
        a = accs[hh][...]
        on = a[0:DA_V_DIM, :] / a[DA_V_DIM:DA_V_DIM + 1, :]
        fin_scr[hh] = on[:, :TQ] - lam * on[:, TQ:]


def _diff_attn_kernel(q_ref, k_ref, v_ref, z_ref, lamv_ref, subw_ref, o_ref,
                      vt_scr, m_scr, acc_scr, p_scr, alpha_scr, fin_scr, *, lambda_init):
    nq = SEQ // TQ
    heads = range(HEADS_PER_STEP)
    for hh in heads:
        vt_scr[hh, 0:DA_V_DIM, :] = v_ref[:, _head_lanes(hh)].astype(F32).T.astype(BF16)
        vt_scr[hh, DA_V_DIM:, :] = jnp.ones((ONES_ROWS, SEQ), BF16)
    lamv = lamv_ref[...]
    lam = (jnp.exp(jnp.sum(lamv[0:1, :] * lamv[1:2, :], keepdims=True))
           - jnp.exp(jnp.sum(lamv[2:3, :] * lamv[3:4, :], keepdims=True)) + lambda_init)
    subw = subw_ref[...] * (1.0 - lambda_init)

    def epilogue(qi):
        rows = pl.ds(_aligned(qi * TQ, TQ), TQ)
        for hh in heads:
            ot = fin_scr[hh]
            ms = jnp.mean(ot * ot, axis=0, keepdims=True)
            ot = ot * lax.rsqrt(ms + EPS) * subw
            z = z_ref[rows, _head_lanes(hh)].astype(F32)
            o_ref[rows, _head_lanes(hh)] = (ot.T * (z * jax.nn.sigmoid(z))).astype(BF16)

    tile = functools.partial(_diff_attn_tile, accs=[acc_scr.at[hh] for hh in heads], lam=lam,
                             q_ref=q_ref, k_ref=k_ref, vt_scr=vt_scr, m_scr=m_scr,
                             p_scr=p_scr, alpha_scr=alpha_scr, fin_scr=fin_scr)
    tile(0, finish_previous=lambda: None)

    def tile_body(qi, carry):
        tile(qi, finish_previous=lambda: epilogue(qi - 1))
        return carry

    lax.fori_loop(1, nq, tile_body, 0)
    epilogue(nq - 1)


def _diff_attn(qkvz, lamv, subw_col, lambda_init):
    kern = functools.partial(_diff_attn_kernel, lambda_init=lambda_init)
    width = HEADS_PER_STEP * LANES
    head_block = lambda which: pl.BlockSpec((None, SEQ, width), lambda b, h: (which, b, h))
    acc_rows = DA_V_DIM + ONES_ROWS
    return pl.pallas_call(
        kern,
        grid=(BATCH, DA_HEADS // HEADS_PER_STEP),
        in_specs=[
            head_block(0), head_block(1), head_block(2), head_block(3),
            pl.BlockSpec((8, LANES), lambda b, h: (0, 0)),
            pl.BlockSpec((DA_V_DIM, 1), lambda b, h: (0, 0)),
        ],
        out_specs=pl.BlockSpec((SEQ, width), lambda b, h: (b, h)),
        out_shape=jax.ShapeDtypeStruct((M_ROWS, D_INNER), BF16),
        scratch_shapes=[
            pltpu.VMEM((HEADS_PER_STEP, acc_rows, SEQ), BF16),
            pltpu.VMEM((HEADS_PER_STEP, 1, 2 * TQ), F32),
            pltpu.VMEM((HEADS_PER_STEP, acc_rows, 2 * TQ), F32),
            pltpu.VMEM((HEADS_PER_STEP, TK, 2 * TQ), BF16),
            pltpu.VMEM((HEADS_PER_STEP, 1, 2 * TQ), F32),
            pltpu.VMEM((HEADS_PER_STEP, DA_V_DIM, TQ), F32),
        ],
        compiler_params=_cparams(("parallel", "parallel")),
    )(qkvz, qkvz, qkvz, qkvz, lamv, subw_col)


def _gla_kernel(q_ref, k_ref, v_ref, z_ref, g_ref, wg_ref, bg_ref, tri_ref, nw_ref,
                o_ref, st_scr):
    t = pl.program_id(1)

    @pl.when(t == 0)
    def _():
        st_scr[...] = jnp.zeros(st_scr.shape, F32)

    nc = GLA_ROWS // GLA_CHUNK
    pre = jnp.dot(g_ref[...].astype(BF16), wg_ref[...], preferred_element_type=F32) + bg_ref[...]
    log_a = (jnp.minimum(pre, 0.0) - jnp.log(1.0 + jnp.exp(-jnp.abs(pre)))) * (1.0 / GLA_GATE_TAU)
    hi = log_a.astype(BF16)
    lo = (log_a - hi.astype(F32)).astype(BF16)
    tri = tri_ref[...]
    bcum = (jnp.dot(tri, hi, preferred_element_type=F32)
            + jnp.dot(tri, lo, preferred_element_type=F32))

    row = lax.broadcasted_iota(jnp.int32, (GLA_ROWS, GLA_ROWS), 0)
    col = lax.broadcasted_iota(jnp.int32, (GLA_ROWS, GLA_ROWS), 1)
    causal = col <= row
    nw = nw_ref[...]
    nt_dims = (((1,), (1,)), ((), ()))

    for h in range(GLA_HEADS):
        ks = slice(h * GLA_HEAD_K, (h + 1) * GLA_HEAD_K)
        vs = slice(h * GLA_HEAD_V, (h + 1) * GLA_HEAD_V)
        bh = bcum[:, ks]
        b_end = [bh[(i + 1) * GLA_CHUNK - 1:(i + 1) * GLA_CHUNK, :] for i in range(nc)]
        def row_factor(e):
            return jnp.broadcast_to(jnp.exp(e), (GLA_CHUNK, GLA_HEAD_K)).astype(BF16)

        q_dec, q_int, k_inv, k_end, k_st = [], [], [], [], []
        for i in range(nc):
            rows = slice(i * GLA_CHUNK, (i + 1) * GLA_CHUNK)
            b = bh[rows, :] if i == 0 else bh[rows, :] - b_end[i - 1]
            qd = q_ref[rows, ks] * (jnp.exp(b) * (GLA_HEAD_K ** -0.5)).astype(BF16)
            ki = k_ref[rows, ks] * jnp.exp(-b).astype(BF16)
            ke = ki * row_factor(b_end[i] if i == 0 else b_end[i] - b_end[i - 1])
            q_dec.append(qd)
            q_int.append(qd if i == 0 else qd * row_factor(b_end[i - 1]))
            k_inv.append(ki)
            k_end.append(ke)
            k_st.append(ke if i == nc - 1 else ke * row_factor(b_end[nc - 1] - b_end[i]))
        attn_rows = []
        for i in range(nc):
            pieces = []
            for j in range(i):
                pieces.append(k_end[j] if j == i - 1
                              else k_end[j] * row_factor(b_end[i - 1] - b_end[j]))
            pieces.append(k_inv[i])
            if i < nc - 1:
                pieces.append(jnp.zeros(((nc - 1 - i) * GLA_CHUNK, GLA_HEAD_K), BF16))
            attn_rows.append(lax.dot_general(q_dec[i], jnp.concatenate(pieces, axis=0), nt_dims,
                                             preferred_element_type=F32))
        attn = jnp.where(causal, jnp.concatenate(attn_rows, axis=0), 0.0).astype(BF16)
        st = st_scr[h]
        v = v_ref[:, vs]
        o = jnp.dot(jnp.concatenate([attn] + [jnp.concatenate(q_int, axis=0)], axis=1),
                    jnp.concatenate([v, st.astype(BF16)], axis=0),
                    preferred_element_type=F32)
        dec_col = jnp.broadcast_to(jnp.exp(b_end[nc - 1]), (LANES, GLA_HEAD_K)).T
        dec = jnp.concatenate([dec_col] * (GLA_HEAD_V // LANES), axis=1)
        st_scr[h] = st * dec + lax.dot_general(
            jnp.concatenate(k_st, axis=0), v, (((0,), (0,)), ((), ())),
            preferred_element_type=F32)
        ms = jnp.mean(o * o, axis=-1, keepdims=True)
        z = z_ref[:, vs]
        gate = (z * jax.nn.sigmoid(z)).astype(F32)
        o_ref[:, vs] = (o * lax.rsqrt(ms + EPS) * nw * gate).astype(BF16)


def _gla(main, glow, wg_bf, bg, tri, nw):
    nt = SEQ // GLA_ROWS
    assert 2 * GLA_DK == D_INNER
    return pl.pallas_call(
        _gla_kernel,
        grid=(BATCH, nt),
        in_specs=[
            pl.BlockSpec((GLA_ROWS, GLA_DK), lambda b, t: (b * nt + t, 0)),
            pl.BlockSpec((GLA_ROWS, GLA_DK), lambda b, t: (b * nt + t, 1)),
            pl.BlockSpec((GLA_ROWS, D_INNER), lambda b, t: (b * nt + t, 1)),
            pl.BlockSpec((GLA_ROWS, D_INNER), lambda b, t: (b * nt + t, 2)),
            pl.BlockSpec((GLA_ROWS, LANES), lambda b, t: (b * nt + t, 0)),
            pl.BlockSpec((LANES, GLA_DK), lambda b, t: (0, 0)),
            pl.BlockSpec((1, GLA_DK), lambda b, t: (0, 0)),
            pl.BlockSpec((GLA_ROWS, GLA_ROWS), lambda b, t: (0, 0)),
            pl.BlockSpec((1, GLA_HEAD_V), lambda b, t: (0, 0)),
        ],
        out_specs=pl.BlockSpec((GLA_ROWS, D_INNER), lambda b, t: (b * nt + t, 0)),
        out_shape=jax.ShapeDtypeStruct((M_ROWS, D_INNER), BF16),
        scratch_shapes=[pltpu.VMEM((GLA_HEADS, GLA_HEAD_K, GLA_HEAD_V), F32)],
        compiler_params=_cparams(("parallel", "arbitrary")),
    )(main, main, main, main, glow, wg_bf, bg, tri, nw)


def _prep_da_w_kernel(w_ref, o_ref, *, n_qk_tiles):
    j = pl.program_id(0)

    @pl.when(j < n_qk_tiles)
    def _():
        lane = lax.broadcasted_iota(jnp.int32, (w_ref.shape[0], LANES), 1)
        take_next = (lane >= ROPE_HALF) & (lane < 2 * ROPE_HALF)
        take_prev = (lane >= 2 * ROPE_HALF) & (lane < 3 * ROPE_HALF)
        for gidx in range(w_ref.shape[1] // LANES):
            cols = slice(gidx * LANES, (gidx + 1) * LANES)
            w = w_ref[:, cols]
            nxt = pltpu.roll(w, LANES - ROPE_HALF, 1)
            prv = pltpu.roll(w, ROPE_HALF, 1)
            o_ref[:, cols] = jnp.where(take_next, nxt,
                                       jnp.where(take_prev, prv, w)).astype(BF16)

    @pl.when(j >= n_qk_tiles)
    def _():
        o_ref[...] = w_ref[...].astype(BF16)


def _prep_da_w(w_in):
    d, n = w_in.shape
    kern = functools.partial(_prep_da_w_kernel, n_qk_tiles=2 * D_INNER // TN_IN)
    return pl.pallas_call(
        kern,
        grid=(n // TN_IN,),
        in_specs=[pl.BlockSpec((d, TN_IN), lambda j: (0, j))],
        out_specs=pl.BlockSpec((d, TN_IN), lambda j: (0, j)),
        out_shape=jax.ShapeDtypeStruct((d, n), BF16),
        compiler_params=_cparams(("parallel",)),
    )(w_in)


def kernel(x, c, positions, ada_w, ada_b, norm_g, da_w_in, da_lam_q1, da_lam_k1, da_lam_q2,
           da_lam_k2, da_subln_w, da_w_out, gla_w_in, gla_w_gate_up, gla_b_gate, gla_norm_w,
           gla_w_out, final_g):
    x2 = x.reshape(M_ROWS, D_MODEL)
    c_pad = jnp.pad(c, ((0, ADA_ROWS - BATCH), (0, 0))).astype(BF16)
    mod = _ada(c_pad, ada_w, ada_b.reshape(DEPTH, 1, 3 * D_MODEL))
    mod = mod.reshape(DEPTH * ADA_ROWS * 3, 1, D_MODEL)
    norm_g3 = norm_g.reshape(DEPTH, 1, D_MODEL)

    inv_freq = ROPE_THETA ** (-jnp.arange(0, DA_HEAD_DIM, 2, dtype=F32) / DA_HEAD_DIM)
    cos, sin = _rope_tables(positions.reshape(BATCH, 1, SEQ), inv_freq.reshape(ROPE_HALF, 1))

    lambda_init = 0.8 - 0.6 * math.exp(-0.3 * 0)
    qkvz = _inproj_da(x2, norm_g3, mod, _prep_da_w(da_w_in[0]), cos, sin, layer=0)
    lamv = jnp.concatenate([da_lam_q1, da_lam_k1, da_lam_q2, da_lam_k2], axis=0)
    lamv = jnp.pad(lamv, ((0, 4), (0, LANES - DA_HEAD_DIM)))
    o = _diff_attn(qkvz, lamv, da_subln_w[0].reshape(DA_V_DIM, 1), lambda_init)
    x2 = _outproj(o, da_w_out[0].astype(BF16), x2, mod, final_g.reshape(1, D_MODEL),
                  layer=0, final_norm=False)

    w = gla_w_in[0]
    wg_in = jnp.pad(w[:, GLA_MAIN:], ((0, 0), (0, LANES - GLA_GATE_RANK))).astype(BF16)
    main, glow = _inproj_gla(x2, norm_g3, mod, w.astype(BF16), wg_in, layer=1)
    wg_up = jnp.pad(gla_w_gate_up[0], ((0, LANES - GLA_GATE_RANK), (0, 0))).astype(BF16)
    r = jnp.arange(GLA_ROWS)
    tri = (r[:, None] >= r[None, :]).astype(BF16)
    o = _gla(main, glow, wg_up, gla_b_gate, tri, gla_norm_w)
    out = _outproj(o, gla_w_out[0].astype(BF16), x2, mod, final_g.reshape(1, D_MODEL),
                   layer=1, final_norm=True)
    return out.reshape(BATCH, SEQ, D_MODEL)
```

```python
import functools
import math

import jax
import jax.numpy as jnp
from jax import lax
from jax.experimental import pallas as pl
from jax.experimental.pallas import tpu as pltpu

F32 = jnp.float32
BF16 = jnp.bfloat16

D_MODEL = 1024
BATCH = 4
SEQ = 4096
DEPTH = 2
D_INNER = 2 * D_MODEL
EPS = 1e-6
M_ROWS = BATCH * SEQ

DA_HEADS = 16
DA_HEAD_DIM = 64
DA_V_DIM = 128
ROPE_THETA = 10000.0
ROPE_HALF = DA_HEAD_DIM // 2

GLA_HEADS = 4
GLA_DK = 1024
GLA_HEAD_K = 256
GLA_HEAD_V = 512
GLA_GATE_RANK = 16
GLA_GATE_TAU = 16.0
GLA_CHUNK = 64
GLA_MAIN = 2 * GLA_DK + 2 * D_INNER

LANES = 128
ADA_ROWS = 16
VMEM_LIMIT = 56 * 1024 * 1024

TM_IN, TN_IN = 2048, 1024
RC_IN = 256
RC_OUT = 256
TM_OUT = 1024
TQ = 1024
TK = 256
NSUB = TQ // TK
CB = 256
assert TQ % TK == 0 and TQ % CB == 0
HEADS_PER_STEP = 2
ONES_ROWS = 16
GLA_ROWS = 256
GLA_BATCH_PER_STEP = 2
TM_ROPE = 1024
TN_ADA = 1536

NEG = -1e30


def _cparams(sem):
    return pltpu.CompilerParams(dimension_semantics=sem, vmem_limit_bytes=VMEM_LIMIT)


def _ada_kernel(c_ref, w_ref, b_ref, o_ref):
    o_ref[...] = jnp.dot(c_ref[...], w_ref[...].astype(BF16),
                         preferred_element_type=F32) + b_ref[...]


def _ada(c_pad, ada_w, ada_b3):
    n = 3 * D_MODEL
    return pl.pallas_call(
        _ada_kernel,
        grid=(DEPTH, n // TN_ADA),
        in_specs=[
            pl.BlockSpec((ADA_ROWS, D_MODEL), lambda l, j: (0, 0)),
            pl.BlockSpec((None, D_MODEL, TN_ADA), lambda l, j: (l, 0, j)),
            pl.BlockSpec((None, 1, TN_ADA), lambda l, j: (l, 0, j)),
        ],
        out_specs=pl.BlockSpec((None, ADA_ROWS, TN_ADA), lambda l, j: (l, 0, j)),
        out_shape=jax.ShapeDtypeStruct((DEPTH, ADA_ROWS, n), F32),
        compiler_params=_cparams(("parallel", "parallel")),
    )(c_pad, ada_w, ada_b3)


def _rope_kernel(pos_ref, invf_ref, cos_ref, sin_ref):
    ang = invf_ref[...] * pos_ref[...].astype(F32)
    c = jnp.cos(ang)
    s = jnp.sin(ang)
    cos_ref[...] = jnp.concatenate([c, c, c, c], axis=0).T
    sin_ref[...] = jnp.concatenate([-s, -s, s, s], axis=0).T


def _rope_tables(pos3, invf):
    nt = SEQ // TM_ROPE
    return pl.pallas_call(
        _rope_kernel,
        grid=(BATCH, nt),
        in_specs=[
            pl.BlockSpec((None, 1, TM_ROPE), lambda b, t: (b, 0, t)),
            pl.BlockSpec((ROPE_HALF, 1), lambda b, t: (0, 0)),
        ],
        out_specs=[
            pl.BlockSpec((TM_ROPE, LANES), lambda b, t: (b * nt + t, 0)),
            pl.BlockSpec((TM_ROPE, LANES), lambda b, t: (b * nt + t, 0)),
        ],
        out_shape=[jax.ShapeDtypeStruct((M_ROWS, LANES), F32)] * 2,
        compiler_params=_cparams(("parallel", "parallel")),
    )(pos3, invf)


def _pipeline_row_chunks(n_chunks, chunk_dot, chunk_store, chunk_prepare=None):
    if chunk_prepare is not None:
        chunk_prepare(0)
    acc = chunk_dot(0)
    for r in range(n_chunks):
        nxt = None
        if r + 1 < n_chunks:
            if chunk_prepare is not None:
                chunk_prepare(r + 1)
            nxt = chunk_dot(r + 1)
        chunk_store(r, acc)
        acc = nxt


def _norm_modulate(x, g, scale, shift):
    ms = jnp.mean(x * x, axis=-1, keepdims=True)
    return x * lax.rsqrt(ms + EPS) * (g * (1.0 + scale)) + shift


def _inproj_da_kernel(x_ref, g_ref, scale_ref, shift_ref, w_ref, cos_ref, sin_ref,
                      o_ref, h_scr, *, n_q_tiles, n_rope_tiles, q_scale):
    j = pl.program_id(1)

    def chunk_norm(r):
        rows = slice(r * RC_IN, (r + 1) * RC_IN)
        h_scr[rows, :] = _norm_modulate(x_ref[rows, :], g_ref[...], scale_ref[...],
                                        shift_ref[...]).astype(BF16)

    is_rope = j < n_rope_tiles
    sc = jnp.where(j < n_q_tiles, q_scale, 1.0).astype(F32)
    cos = jnp.where(is_rope, cos_ref[...] * sc, 1.0)
    sin = jnp.where(is_rope, sin_ref[...] * sc, 0.0)
    n_chunks = TM_IN // RC_IN

    def chunk_dot(r):
        return jnp.dot(h_scr[r * RC_IN:(r + 1) * RC_IN, :], w_ref[...],
                       preferred_element_type=F32)

    def chunk_store(r, acc):
        rows = slice(r * RC_IN, (r + 1) * RC_IN)
        for gidx in range(acc.shape[1] // LANES):
            cols = slice(gidx * LANES, (gidx + 1) * LANES)
            t = acc[:, cols]
            o_ref[rows, cols] = (t * cos[rows, :]
                                 + pltpu.roll(t, LANES // 2, 1) * sin[rows, :]).astype(BF16)

    @pl.when(j == 0)
    def _():
        _pipeline_row_chunks(n_chunks, chunk_dot, chunk_store, chunk_norm)

    @pl.when(j > 0)
    def _():
        _pipeline_row_chunks(n_chunks, chunk_dot, chunk_store)


def _inproj_da(x2, norm_g, mod, w_bf, cos, sin, layer):
    n = w_bf.shape[1]
    seg_tiles = D_INNER // TN_IN
    tiles_per_batch = SEQ // TM_IN

    def mod_idx(which):
        return lambda i, j: ((layer * ADA_ROWS + i // tiles_per_batch) * 3 + which, 0, 0)

    kern = functools.partial(_inproj_da_kernel, n_q_tiles=seg_tiles,
                             n_rope_tiles=2 * seg_tiles,
                             q_scale=DA_HEAD_DIM ** -0.5 * math.log2(math.e))
    return pl.pallas_call(
        kern,
        grid=(M_ROWS // TM_IN, n // TN_IN),
        in_specs=[
            pl.BlockSpec((TM_IN, D_MODEL), lambda i, j: (i, 0)),
            pl.BlockSpec((None, 1, D_MODEL), lambda i, j: (layer, 0, 0)),
            pl.BlockSpec((None, 1, D_MODEL), mod_idx(1)),
            pl.BlockSpec((None, 1, D_MODEL), mod_idx(0)),
            pl.BlockSpec((D_MODEL, TN_IN), lambda i, j: (0, j)),
            pl.BlockSpec((TM_IN, LANES), lambda i, j: (i, 0)),
            pl.BlockSpec((TM_IN, LANES), lambda i, j: (i, 0)),
        ],
        out_specs=pl.BlockSpec((None, TM_IN, TN_IN),
                               lambda i, j: (j // seg_tiles, i, j % seg_tiles)),
        out_shape=jax.ShapeDtypeStruct((4, M_ROWS, D_INNER), BF16),
        scratch_shapes=[pltpu.VMEM((TM_IN, D_MODEL), BF16)],
        compiler_params=_cparams(("parallel", "arbitrary")),
    )(x2, norm_g, mod, mod, w_bf, cos, sin)


def _inproj_gla_kernel(x_ref, g_ref, scale_ref, shift_ref, w_ref, wg_ref,
                       o_ref, glow_ref, h_scr):
    j = pl.program_id(1)
    n_chunks = TM_IN // RC_IN

    def chunk_norm(r):
        rows = slice(r * RC_IN, (r + 1) * RC_IN)
        h = _norm_modulate(x_ref[rows, :], g_ref[...], scale_ref[...],
                           shift_ref[...]).astype(BF16)
        h_scr[rows, :] = h
        glow_ref[rows, :] = jnp.dot(h, wg_ref[...], preferred_element_type=F32)

    def chunk_dot(r):
        return jnp.dot(h_scr[r * RC_IN:(r + 1) * RC_IN, :], w_ref[...],
                       preferred_element_type=F32)

    def chunk_store(r, acc):
        o_ref[r * RC_IN:(r + 1) * RC_IN, :] = acc.astype(BF16)

    @pl.when(j == 0)
    def _():
        _pipeline_row_chunks(n_chunks, chunk_dot, chunk_store, chunk_norm)

    @pl.when(j > 0)
    def _():
        _pipeline_row_chunks(n_chunks, chunk_dot, chunk_store)


def _inproj_gla(x2, norm_g, mod, w_bf, wg_bf, layer):
    n = GLA_MAIN
    tiles_per_batch = SEQ // TM_IN

    def mod_idx(which):
        return lambda i, j: ((layer * ADA_ROWS + i // tiles_per_batch) * 3 + which, 0, 0)

    return pl.pallas_call(
        _inproj_gla_kernel,
        grid=(M_ROWS // TM_IN, n // TN_IN),
        in_specs=[
            pl.BlockSpec((TM_IN, D_MODEL), lambda i, j: (i, 0)),
            pl.BlockSpec((None, 1, D_MODEL), lambda i, j: (layer, 0, 0)),
            pl.BlockSpec((None, 1, D_MODEL), mod_idx(1)),
            pl.BlockSpec((None, 1, D_MODEL), mod_idx(0)),
            pl.BlockSpec((D_MODEL, TN_IN), lambda i, j: (0, j)),
            pl.BlockSpec((D_MODEL, LANES), lambda i, j: (0, 0)),
        ],
        out_specs=[
            pl.BlockSpec((TM_IN, TN_IN), lambda i, j: (i, j)),
            pl.BlockSpec((TM_IN, LANES), lambda i, j: (i, 0)),
        ],
        out_shape=[jax.ShapeDtypeStruct((M_ROWS, n), BF16),
                   jax.ShapeDtypeStruct((M_ROWS, LANES), F32)],
        scratch_shapes=[pltpu.VMEM((TM_IN, D_MODEL), BF16)],
        compiler_params=_cparams(("parallel", "arbitrary")),
    )(x2, norm_g, mod, mod, w_bf, wg_bf)


def _outproj_kernel(o_ref, w_ref, x_ref, gate_ref, fg_ref, out_ref, *, final_norm):
    def chunk_dot(r):
        return jnp.dot(o_ref[r * RC_OUT:(r + 1) * RC_OUT, :], w_ref[...],
                       preferred_element_type=F32)

    def chunk_store(r, y):
        rows = slice(r * RC_OUT, (r + 1) * RC_OUT)
        x = x_ref[rows, :] + gate_ref[...] * y
        if final_norm:
            ms = jnp.mean(x * x, axis=-1, keepdims=True)
            x = x * lax.rsqrt(ms + EPS) * fg_ref[...]
        out_ref[rows, :] = x

    _pipeline_row_chunks(TM_OUT // RC_OUT, chunk_dot, chunk_store)


def _outproj(o, w_bf, x2, mod, final_g, layer, final_norm):
    tiles_per_batch = SEQ // TM_OUT
    kern = functools.partial(_outproj_kernel, final_norm=final_norm)
    return pl.pallas_call(
        kern,
        grid=(M_ROWS // TM_OUT,),
        in_specs=[
            pl.BlockSpec((TM_OUT, D_INNER), lambda i: (i, 0)),
            pl.BlockSpec((D_INNER, D_MODEL), lambda i: (0, 0)),
            pl.BlockSpec((TM_OUT, D_MODEL), lambda i: (i, 0)),
            pl.BlockSpec((None, 1, D_MODEL),
                         lambda i: ((layer * ADA_ROWS + i // tiles_per_batch) * 3 + 2, 0, 0)),
            pl.BlockSpec((1, D_MODEL), lambda i: (0, 0)),
        ],
        out_specs=pl.BlockSpec((TM_OUT, D_MODEL), lambda i: (i, 0)),
        out_shape=jax.ShapeDtypeStruct((M_ROWS, D_MODEL), F32),
        compiler_params=_cparams(("parallel",)),
    )(o, w_bf, x2, mod, final_g)


def _aligned(start, multiple):
    return start if isinstance(start, int) else pl.multiple_of(start, multiple)


def _head_lanes(hh):
    return slice(hh * LANES, (hh + 1) * LANES)


def _diff_attn_tile(qi, *, finish_previous, accs, lam, q_ref, k_ref, vt_scr, m_scr, p_scr,
                    alpha_scr, fin_scr):
    heads = range(len(accs))
    lane = lax.broadcasted_iota(jnp.int32, (TQ, LANES), 1)
    is_a = (lane % DA_HEAD_DIM) < ROPE_HALF
    qs = []
    for hh in heads:
        q = q_ref[pl.ds(_aligned(qi * TQ, TQ), TQ), _head_lanes(hh)]
        zero = jnp.zeros_like(q)
        qs.append(jnp.concatenate([jnp.where(is_a, q, zero), jnp.where(is_a, zero, q)],
                                  axis=0))
        accs[hh][...] = jnp.zeros(accs[hh].shape, F32)
    m_scr[...] = jnp.full(m_scr.shape, NEG, F32)

    def deferred_pv(hh, prev_start, cs):
        vt = vt_scr[hh, :, pl.ds(_aligned(prev_start, TK), TK)]
        accs[hh][:, cs] = alpha_scr[hh, :, cs] * accs[hh][:, cs] + jnp.dot(
            vt, p_scr[hh, :, cs], preferred_element_type=F32)

    def block_masked_out(diag_offset, c):
        return diag_offset is not None and diag_offset > (c * CB) % TQ + CB - 1

    def key_round(start, diag_offset, prev_start, prev_diag_offset=None):
        ks = [k_ref[pl.ds(_aligned(start, TK), TK), _head_lanes(hh)] for hh in heads]
        for c in range(2 * TQ // CB):
            cs = slice(c * CB, (c + 1) * CB)
            q_lo = (c * CB) % TQ
            masked_out = block_masked_out(diag_offset, c)
            needs_mask = (diag_offset is not None and not masked_out
                          and diag_offset + TK - 1 > q_lo)
            for hh in heads:
                if not masked_out:
                    st = lax.dot_general(ks[hh], qs[hh][cs], (((1,), (1,)), ((), ())),
                                         preferred_element_type=F32)
                if prev_start is not None and not block_masked_out(prev_diag_offset, c):
                    deferred_pv(hh, prev_start, cs)
                if masked_out:
                    alpha_scr[hh, :, cs] = jnp.ones((1, CB), F32)
                    p_scr[hh, :, cs] = jnp.zeros((TK, CB), BF16)
                    continue
                if needs_mask:
                    kpos = lax.broadcasted_iota(jnp.int32, st.shape, 0) + diag_offset
                    qpos = lax.broadcasted_iota(jnp.int32, st.shape, 1) + q_lo
                    st = jnp.where(kpos <= qpos, st, NEG)
                sb = st.astype(BF16)
                m_prev = m_scr[hh, :, cs]
                m_new = jnp.maximum(m_prev, jnp.max(sb, axis=0, keepdims=True).astype(F32))
                alpha_scr[hh, :, cs] = jnp.exp2(m_prev - m_new)
                p_scr[hh, :, cs] = jnp.exp2(sb - m_new.astype(BF16))
                m_scr[hh, :, cs] = m_new

    diag = qi * TQ
    order = list(range(NSUB - 1, -1, -1))
    key_round(diag + order[0] * TK, order[0] * TK, None)
    for prev, u in zip(order[:-1], order[1:]):
        key_round(diag + u * TK, u * TK, diag + prev * TK, prev * TK)
    diag_last = diag + order[-1] * TK

    def body(kc, carry):
        base = kc * TQ
        key_round(base, None, jnp.where(kc == 0, diag_last, base - TK))
        for u in range(1, NSUB):
            key_round(base + u * TK, None, base + (u - 1) * TK)
        return carry

    finish_previous()
    if not (isinstance(qi, int) and qi == 0):
        lax.fori_loop(0, qi, body, 0)
    last_start = diag_last if isinstance(qi, int) and qi == 0 else diag - TK
    for c in range(2 * TQ // CB):
        for hh in heads:
            deferred_pv(hh, last_start, slice(c * CB, (c + 1) * CB))
    for hh in heads:
        a = accs[hh][...]
        on = a[0:DA_V_DIM, :] / a[DA_V_DIM:DA_V_DIM + 1, :]
        fin_scr[hh] = on[:, :TQ] - lam * on[:, TQ:]


def _diff_attn_kernel(q_ref, k_ref, v_ref, z_ref, lamv_ref, subw_ref, o_ref,
                      vt_scr, m_scr, acc_scr, p_scr, alpha_scr, fin_scr, *, lambda_init):
    nq = SEQ // TQ
    heads = range(HEADS_PER_STEP)
    for hh in heads:
        vt_scr[hh, 0:DA_V_DIM, :] = v_ref[:, _head_lanes(hh)].astype(F32).T.astype(BF16)
        vt_scr[hh, DA_V_DIM:, :] = jnp.ones((ONES_ROWS, SEQ), BF16)
    lamv = lamv_ref[...]
    lam = (jnp.exp(jnp.sum(lamv[0:1, :] * lamv[1:2, :], keepdims=True))
           - jnp.exp(jnp.sum(lamv[2:3, :] * lamv[3:4, :], keepdims=True)) + lambda_init)
    subw = subw_ref[...] * (1.0 - lambda_init)

    def epilogue(qi):
        rows = pl.ds(_aligned(qi * TQ, TQ), TQ)
        for hh in heads:
            ot = fin_scr[hh]
            ms = jnp.mean(ot * ot, axis=0, keepdims=True)
            ot = ot * lax.rsqrt(ms + EPS) * subw
            z = z_ref[rows, _head_lanes(hh)].astype(F32)
            o_ref[rows, _head_lanes(hh)] = (ot.T * (z * jax.nn.sigmoid(z))).astype(BF16)

    tile = functools.partial(_diff_attn_tile, accs=[acc_scr.at[hh] for hh in heads], lam=lam,
                             q_ref=q_ref, k_ref=k_ref, vt_scr=vt_scr, m_scr=m_scr,
                             p_scr=p_scr, alpha_scr=alpha_scr, fin_scr=fin_scr)
    tile(0, finish_previous=lambda: None)

    def tile_body(qi, carry):
        tile(qi, finish_previous=lambda: epilogue(qi - 1))
        return carry

    lax.fori_loop(1, nq, tile_body, 0)
    epilogue(nq - 1)


def _diff_attn(qkvz, lamv, subw_col, lambda_init):
    kern = functools.partial(_diff_attn_kernel, lambda_init=lambda_init)
    width = HEADS_PER_STEP * LANES
    head_block = lambda which: pl.BlockSpec((None, SEQ, width), lambda b, h: (which, b, h))
    acc_rows = DA_V_DIM + ONES_ROWS
    return pl.pallas_call(
        kern,
        grid=(BATCH, DA_HEADS // HEADS_PER_STEP),
        in_specs=[
            head_block(0), head_block(1), head_block(2), head_block(3),
            pl.BlockSpec((8, LANES), lambda b, h: (0, 0)),
            pl.BlockSpec((DA_V_DIM, 1), lambda b, h: (0, 0)),
        ],
        out_specs=pl.BlockSpec((SEQ, width), lambda b, h: (b, h)),
        out_shape=jax.ShapeDtypeStruct((M_ROWS, D_INNER), BF16),
        scratch_shapes=[
            pltpu.VMEM((HEADS_PER_STEP, acc_rows, SEQ), BF16),
            pltpu.VMEM((HEADS_PER_STEP, 1, 2 * TQ), F32),
            pltpu.VMEM((HEADS_PER_STEP, acc_rows, 2 * TQ), F32),
            pltpu.VMEM((HEADS_PER_STEP, TK, 2 * TQ), BF16),
            pltpu.VMEM((HEADS_PER_STEP, 1, 2 * TQ), F32),
            pltpu.VMEM((HEADS_PER_STEP, DA_V_DIM, TQ), F32),
        ],
        compiler_params=_cparams(("parallel", "parallel")),
    )(qkvz, qkvz, qkvz, qkvz, lamv, subw_col)


def _gla_kernel(q_ref, k_ref, v_ref, z_ref, g_ref, wg_ref, bg_ref, tri_ref, nw_ref,
                o_ref, st_scr):
    t = pl.program_id(1)

    @pl.when(t == 0)
    def _():
        st_scr[...] = jnp.zeros(st_scr.shape, F32)

    nc = GLA_ROWS // GLA_CHUNK
    tri = tri_ref[...]
    bcums = []
    for bb in range(GLA_BATCH_PER_STEP):
        pre = (jnp.dot(g_ref[bb].astype(BF16), wg_ref[...], preferred_element_type=F32)
               + bg_ref[...])
        log_a = ((jnp.minimum(pre, 0.0) - jnp.log(1.0 + jnp.exp(-jnp.abs(pre))))
                 * (1.0 / GLA_GATE_TAU))
        hi = log_a.astype(BF16)
        lo = (log_a - hi.astype(F32)).astype(BF16)
        bcums.append(jnp.dot(tri, hi, preferred_element_type=F32)
                     + jnp.dot(tri, lo, preferred_element_type=F32))

    row = lax.broadcasted_iota(jnp.int32, (GLA_ROWS, GLA_ROWS), 0)
    col = lax.broadcasted_iota(jnp.int32, (GLA_ROWS, GLA_ROWS), 1)
    causal = col <= row
    nw = nw_ref[...]
    nt_dims = (((1,), (1,)), ((), ()))

    for h, bb in [(h, bb) for h in range(GLA_HEADS) for bb in range(GLA_BATCH_PER_STEP)]:
        ks = slice(h * GLA_HEAD_K, (h + 1) * GLA_HEAD_K)
        vs = slice(h * GLA_HEAD_V, (h + 1) * GLA_HEAD_V)
        bh = bcums[bb][:, ks]
        b_end = [bh[(i + 1) * GLA_CHUNK - 1:(i + 1) * GLA_CHUNK, :] for i in range(nc)]
        def row_factor(e):
            return jnp.broadcast_to(jnp.exp(e), (GLA_CHUNK, GLA_HEAD_K)).astype(BF16)

        q_dec, q_int, k_inv, k_end, k_st = [], [], [], [], []
        for i in range(nc):
            rows = slice(i * GLA_CHUNK, (i + 1) * GLA_CHUNK)
            b = bh[rows, :] if i == 0 else bh[rows, :] - b_end[i - 1]
            qd = q_ref[bb, rows, ks] * (jnp.exp(b) * (GLA_HEAD_K ** -0.5)).astype(BF16)
            ki = k_ref[bb, rows, ks] * jnp.exp(-b).astype(BF16)
            ke = ki * row_factor(b_end[i] if i == 0 else b_end[i] - b_end[i - 1])
            q_dec.append(qd)
            q_int.append(qd if i == 0 else qd * row_factor(b_end[i - 1]))
            k_inv.append(ki)
            k_end.append(ke)
            k_st.append(ke if i == nc - 1 else ke * row_factor(b_end[nc - 1] - b_end[i]))
        attn_rows = []
        for i in range(nc):
            pieces = []
            for j in range(i):
                pieces.append(k_end[j] if j == i - 1
                              else k_end[j] * row_factor(b_end[i - 1] - b_end[j]))
            pieces.append(k_inv[i])
            if i < nc - 1:
                pieces.append(jnp.zeros(((nc - 1 - i) * GLA_CHUNK, GLA_HEAD_K), BF16))
            attn_rows.append(lax.dot_general(q_dec[i], jnp.concatenate(pieces, axis=0), nt_dims,
                                             preferred_element_type=F32))
        attn = jnp.where(causal, jnp.concatenate(attn_rows, axis=0), 0.0).astype(BF16)
        st = st_scr[bb, h]
        v = v_ref[bb, :, vs]
        o = jnp.dot(jnp.concatenate([attn] + [jnp.concatenate(q_int, axis=0)], axis=1),
                    jnp.concatenate([v, st.astype(BF16)], axis=0),
                    preferred_element_type=F32)
        dec_col = jnp.broadcast_to(jnp.exp(b_end[nc - 1]), (LANES, GLA_HEAD_K)).T
        dec = jnp.concatenate([dec_col] * (GLA_HEAD_V // LANES), axis=1)
        st_scr[bb, h] = st * dec + lax.dot_general(
            jnp.concatenate(k_st, axis=0), v, (((0,), (0,)), ((), ())),
            preferred_element_type=F32)
        ms = jnp.mean(o * o, axis=-1, keepdims=True)
        z = z_ref[bb, :, vs]
        gate = (z * jax.nn.sigmoid(z)).astype(F32)
        o_ref[bb, :, vs] = (o * lax.rsqrt(ms + EPS) * nw * gate).astype(BF16)


def _gla(main, glow, wg_bf, bg, tri, nw):
    nt = SEQ // GLA_ROWS
    bps = GLA_BATCH_PER_STEP
    assert 2 * GLA_DK == D_INNER and BATCH % bps == 0
    main3 = main.reshape(BATCH, SEQ, main.shape[1])
    glow3 = glow.reshape(BATCH, SEQ, LANES)
    out = pl.pallas_call(
        _gla_kernel,
        grid=(BATCH // bps, nt),
        in_specs=[
            pl.BlockSpec((bps, GLA_ROWS, GLA_DK), lambda b, t: (b, t, 0)),
            pl.BlockSpec((bps, GLA_ROWS, GLA_DK), lambda b, t: (b, t, 1)),
            pl.BlockSpec((bps, GLA_ROWS, D_INNER), lambda b, t: (b, t, 1)),
            pl.BlockSpec((bps, GLA_ROWS, D_INNER), lambda b, t: (b, t, 2)),
            pl.BlockSpec((bps, GLA_ROWS, LANES), lambda b, t: (b, t, 0)),
            pl.BlockSpec((LANES, GLA_DK), lambda b, t: (0, 0)),
            pl.BlockSpec((1, GLA_DK), lambda b, t: (0, 0)),
            pl.BlockSpec((GLA_ROWS, GLA_ROWS), lambda b, t: (0, 0)),
            pl.BlockSpec((1, GLA_HEAD_V), lambda b, t: (0, 0)),
        ],
        out_specs=pl.BlockSpec((bps, GLA_ROWS, D_INNER), lambda b, t: (b, t, 0)),
        out_shape=jax.ShapeDtypeStruct((BATCH, SEQ, D_INNER), BF16),
        scratch_shapes=[pltpu.VMEM((bps, GLA_HEADS, GLA_HEAD_K, GLA_HEAD_V), F32)],
        compiler_params=_cparams(("parallel", "arbitrary")),
    )(main3, main3, main3, main3, glow3, wg_bf, bg, tri, nw)
    return out.reshape(M_ROWS, D_INNER)


def _prep_da_w_kernel(w_ref, o_ref, *, n_qk_tiles):
    j = pl.program_id(0)

    @pl.when(j < n_qk_tiles)
    def _():
        lane = lax.broadcasted_iota(jnp.int32, (w_ref.shape[0], LANES), 1)
        take_next = (lane >= ROPE_HALF) & (lane < 2 * ROPE_HALF)
        take_prev = (lane >= 2 * ROPE_HALF) & (lane < 3 * ROPE_HALF)
        for gidx in range(w_ref.shape[1] // LANES):
            cols = slice(gidx * LANES, (gidx + 1) * LANES)
            w = w_ref[:, cols]
            nxt = pltpu.roll(w, LANES - ROPE_HALF, 1)
            prv = pltpu.roll(w, ROPE_HALF, 1)
            o_ref[:, cols] = jnp.where(take_next, nxt,
                                       jnp.where(take_prev, prv, w)).astype(BF16)

    @pl.when(j >= n_qk_tiles)
    def _():
        o_ref[...] = w_ref[...].astype(BF16)


def _prep_da_w(w_in):
    d, n = w_in.shape
    kern = functools.partial(_prep_da_w_kernel, n_qk_tiles=2 * D_INNER // TN_IN)
    return pl.pallas_call(
        kern,
        grid=(n // TN_IN,),
        in_specs=[pl.BlockSpec((d, TN_IN), lambda j: (0, j))],
        out_specs=pl.BlockSpec((d, TN_IN), lambda j: (0, j)),
        out_shape=jax.ShapeDtypeStruct((d, n), BF16),
        compiler_params=_cparams(("parallel",)),
    )(w_in)


def kernel(x, c, positions, ada_w, ada_b, norm_g, da_w_in, da_lam_q1, da_lam_k1, da_lam_q2,
           da_lam_k2, da_subln_w, da_w_out, gla_w_in, gla_w_gate_up, gla_b_gate, gla_norm_w,
           gla_w_out, final_g):
    x2 = x.reshape(M_ROWS, D_MODEL)
    c_pad = jnp.pad(c, ((0, ADA_ROWS - BATCH), (0, 0))).astype(BF16)
    mod = _ada(c_pad, ada_w, ada_b.reshape(DEPTH, 1, 3 * D_MODEL))
    mod = mod.reshape(DEPTH * ADA_ROWS * 3, 1, D_MODEL)
    norm_g3 = norm_g.reshape(DEPTH, 1, D_MODEL)

    inv_freq = ROPE_THETA ** (-jnp.arange(0, DA_HEAD_DIM, 2, dtype=F32) / DA_HEAD_DIM)
    cos, sin = _rope_tables(positions.reshape(BATCH, 1, SEQ), inv_freq.reshape(ROPE_HALF, 1))

    lambda_init = 0.8 - 0.6 * math.exp(-0.3 * 0)
    qkvz = _inproj_da(x2, norm_g3, mod, _prep_da_w(da_w_in[0]), cos, sin, layer=0)
    lamv = jnp.concatenate([da_lam_q1, da_lam_k1, da_lam_q2, da_lam_k2], axis=0)
    lamv = jnp.pad(lamv, ((0, 4), (0, LANES - DA_HEAD_DIM)))
    o = _diff_attn(qkvz, lamv, da_subln_w[0].reshape(DA_V_DIM, 1), lambda_init)
    x2 = _outproj(o, da_w_out[0].astype(BF16), x2, mod, final_g.reshape(1, D_MODEL),
                  layer=0, final_norm=False)

    w = gla_w_in[0]
    wg_in = jnp.pad(w[:, GLA_MAIN:], ((0, 0), (0, LANES - GLA_GATE_RANK))).astype(BF16)
    main, glow = _inproj_gla(x2, norm_g3, mod, w.astype(BF16), wg_in, layer=1)
    wg_up = jnp.pad(gla_w_gate_up[0], ((0, LANES - GLA_GATE_RANK), (0, 0))).astype(BF16)
    r = jnp.arange(GLA_ROWS)
    tri = (r[:, None] >= r[None, :]).astype(BF16)
    o = _gla(main, glow, wg_up, gla_b_gate, tri, gla_norm_w)
    out = _outproj(o, gla_w_out[0].astype(BF16), x2, mod, final_g.reshape(1, D_MODEL),
                   layer=1, final_norm=True)
    return out.reshape(BATCH, SEQ, D_MODEL)
```

```python
import functools
import math

import jax
import jax.numpy as jnp
from jax import lax
from jax.experimental import pallas as pl
from jax.experimental.pallas import tpu as pltpu

F32 = jnp.float32
BF16 = jnp.bfloat16

D_MODEL = 1024
BATCH = 4
SEQ = 4096
DEPTH = 2
D_INNER = 2 * D_MODEL
EPS = 1e-6
M_ROWS = BATCH * SEQ

DA_HEADS = 16
DA_HEAD_DIM = 64
DA_V_DIM = 128
ROPE_THETA = 10000.0
ROPE_HALF = DA_HEAD_DIM // 2

GLA_HEADS = 4
GLA_DK = 1024
GLA_HEAD_K = 256
GLA_HEAD_V = 512
GLA_GATE_RANK = 16
GLA_GATE_TAU = 16.0
GLA_CHUNK = 64
GLA_MAIN = 2 * GLA_DK + 2 * D_INNER

LANES = 128
ADA_ROWS = 16
VMEM_LIMIT = 56 * 1024 * 1024

TM_IN, TN_IN = 2048, 1024
RC_IN = 256
RC_OUT = 256
TM_OUT = 1024
TQ = 1024
TK = 256
NSUB = TQ // TK
CB = 256
assert TQ % TK == 0 and TQ % CB == 0
HEADS_PER_STEP = 2
ONES_ROWS = 16
GLA_ROWS = 256
GLA_BATCH_PER_STEP = 2
TM_ROPE = 1024
TN_ADA = 1536

NEG = -1e30


def _cparams(sem):
    return pltpu.CompilerParams(dimension_semantics=sem, vmem_limit_bytes=VMEM_LIMIT)


def _ada_kernel(c_ref, w_ref, b_ref, o_ref):
    o_ref[...] = jnp.dot(c_ref[...], w_ref[...].astype(BF16),
                         preferred_element_type=F32) + b_ref[...]


def _ada(c_pad, ada_w, ada_b3):
    n = 3 * D_MODEL
    return pl.pallas_call(
        _ada_kernel,
        grid=(DEPTH, n // TN_ADA),
        in_specs=[
            pl.BlockSpec((ADA_ROWS, D_MODEL), lambda l, j: (0, 0)),
            pl.BlockSpec((None, D_MODEL, TN_ADA), lambda l, j: (l, 0, j)),
            pl.BlockSpec((None, 1, TN_ADA), lambda l, j: (l, 0, j)),
        ],
        out_specs=pl.BlockSpec((None, ADA_ROWS, TN_ADA), lambda l, j: (l, 0, j)),
        out_shape=jax.ShapeDtypeStruct((DEPTH, ADA_ROWS, n), F32),
        compiler_params=_cparams(("parallel", "parallel")),
    )(c_pad, ada_w, ada_b3)


def _rope_kernel(pos_ref, invf_ref, cos_ref, sin_ref):
    ang = invf_ref[...] * pos_ref[...].astype(F32)
    c = jnp.cos(ang)
    s = jnp.sin(ang)
    cos_ref[...] = jnp.concatenate([c, c, c, c], axis=0).T
    sin_ref[...] = jnp.concatenate([-s, -s, s, s], axis=0).T


def _rope_tables(pos3, invf):
    nt = SEQ // TM_ROPE
    return pl.pallas_call(
        _rope_kernel,
        grid=(BATCH, nt),
        in_specs=[
            pl.BlockSpec((None, 1, TM_ROPE), lambda b, t: (b, 0, t)),
            pl.BlockSpec((ROPE_HALF, 1), lambda b, t: (0, 0)),
        ],
        out_specs=[
            pl.BlockSpec((TM_ROPE, LANES), lambda b, t: (b * nt + t, 0)),
            pl.BlockSpec((TM_ROPE, LANES), lambda b, t: (b * nt + t, 0)),
        ],
        out_shape=[jax.ShapeDtypeStruct((M_ROWS, LANES), F32)] * 2,
        compiler_params=_cparams(("parallel", "parallel")),
    )(pos3, invf)


def _pipeline_row_chunks(n_chunks, chunk_dot, chunk_store, chunk_prepare=None):
    if chunk_prepare is not None:
        chunk_prepare(0)
    acc = chunk_dot(0)
    for r in range(n_chunks):
        nxt = None
        if r + 1 < n_chunks:
            if chunk_prepare is not None:
                chunk_prepare(r + 1)
            nxt = chunk_dot(r + 1)
        chunk_store(r, acc)
        acc = nxt


def _norm_modulate(x, g, scale, shift):
    ms = jnp.mean(x * x, axis=-1, keepdims=True)
    return x * lax.rsqrt(ms + EPS) * (g * (1.0 + scale)) + shift


def _inproj_da_kernel(x_ref, g_ref, scale_ref, shift_ref, w_ref, cos_ref, sin_ref,
                      o_ref, h_scr, *, n_q_tiles, n_rope_tiles, q_scale):
    j = pl.program_id(1)

    @pl.when(j == 0)
    def _():
        for r in range(TM_IN // RC_IN):
            rows = slice(r * RC_IN, (r + 1) * RC_IN)
            h_scr[rows, :] = _norm_modulate(x_ref[rows, :], g_ref[...], scale_ref[...],
                                            shift_ref[...]).astype(BF16)

    is_rope = j < n_rope_tiles
    sc = jnp.where(j < n_q_tiles, q_scale, 1.0).astype(F32)
    cos = jnp.where(is_rope, cos_ref[...] * sc, 1.0)
    sin = jnp.where(is_rope, sin_ref[...] * sc, 0.0)
    n_chunks = TM_IN // RC_IN

    def chunk_dot(r):
        return jnp.dot(h_scr[r * RC_IN:(r + 1) * RC_IN, :], w_ref[...],
                       preferred_element_type=F32)

    def chunk_store(r, acc):
        rows = slice(r * RC_IN, (r + 1) * RC_IN)
        for gidx in range(acc.shape[1] // LANES):
            cols = slice(gidx * LANES, (gidx + 1) * LANES)
            t = acc[:, cols]
            o_ref[rows, cols] = (t * cos[rows, :]
                                 + pltpu.roll(t, LANES // 2, 1) * sin[rows, :]).astype(BF16)

    _pipeline_row_chunks(n_chunks, chunk_dot, chunk_store)


def _inproj_da(x2, norm_g, mod, w_bf, cos, sin, layer):
    n = w_bf.shape[1]
    seg_tiles = D_INNER // TN_IN
    tiles_per_batch = SEQ // TM_IN

    def mod_idx(which):
        return lambda i, j: ((layer * ADA_ROWS + i // tiles_per_batch) * 3 + which, 0, 0)

    kern = functools.partial(_inproj_da_kernel, n_q_tiles=seg_tiles,
                             n_rope_tiles=2 * seg_tiles,
                             q_scale=DA_HEAD_DIM ** -0.5 * math.log2(math.e))
    return pl.pallas_call(
        kern,
        grid=(M_ROWS // TM_IN, n // TN_IN),
        in_specs=[
            pl.BlockSpec((TM_IN, D_MODEL), lambda i, j: (i, 0)),
            pl.BlockSpec((None, 1, D_MODEL), lambda i, j: (layer, 0, 0)),
            pl.BlockSpec((None, 1, D_MODEL), mod_idx(1)),
            pl.BlockSpec((None, 1, D_MODEL), mod_idx(0)),
            pl.BlockSpec((D_MODEL, TN_IN), lambda i, j: (0, j)),
            pl.BlockSpec((TM_IN, LANES), lambda i, j: (i, 0)),
            pl.BlockSpec((TM_IN, LANES), lambda i, j: (i, 0)),
        ],
        out_specs=pl.BlockSpec((None, TM_IN, TN_IN),
                               lambda i, j: (j // seg_tiles, i, j % seg_tiles)),
        out_shape=jax.ShapeDtypeStruct((4, M_ROWS, D_INNER), BF16),
        scratch_shapes=[pltpu.VMEM((TM_IN, D_MODEL), BF16)],
        compiler_params=_cparams(("parallel", "arbitrary")),
    )(x2, norm_g, mod, mod, w_bf, cos, sin)


def _inproj_gla_kernel(x_ref, g_ref, scale_ref, shift_ref, w_ref, wg_ref,
                       o_ref, glow_ref, h_scr):
    j = pl.program_id(1)
    n_chunks = TM_IN // RC_IN

    def chunk_norm(r):
        rows = slice(r * RC_IN, (r + 1) * RC_IN)
        h = _norm_modulate(x_ref[rows, :], g_ref[...], scale_ref[...],
                           shift_ref[...]).astype(BF16)
        h_scr[rows, :] = h
        glow_ref[rows, :] = jnp.dot(h, wg_ref[...], preferred_element_type=F32)

    def chunk_dot(r):
        return jnp.dot(h_scr[r * RC_IN:(r + 1) * RC_IN, :], w_ref[...],
                       preferred_element_type=F32)

    def chunk_store(r, acc):
        o_ref[r * RC_IN:(r + 1) * RC_IN, :] = acc.astype(BF16)

    @pl.when(j == 0)
    def _():
        _pipeline_row_chunks(n_chunks, chunk_dot, chunk_store, chunk_norm)

    @pl.when(j > 0)
    def _():
        _pipeline_row_chunks(n_chunks, chunk_dot, chunk_store)


def _inproj_gla(x2, norm_g, mod, w_bf, wg_bf, layer):
    n = GLA_MAIN
    tiles_per_batch = SEQ // TM_IN

    def mod_idx(which):
        return lambda i, j: ((layer * ADA_ROWS + i // tiles_per_batch) * 3 + which, 0, 0)

    return pl.pallas_call(
        _inproj_gla_kernel,
        grid=(M_ROWS // TM_IN, n // TN_IN),
        in_specs=[
            pl.BlockSpec((TM_IN, D_MODEL), lambda i, j: (i, 0)),
            pl.BlockSpec((None, 1, D_MODEL), lambda i, j: (layer, 0, 0)),
            pl.BlockSpec((None, 1, D_MODEL), mod_idx(1)),
            pl.BlockSpec((None, 1, D_MODEL), mod_idx(0)),
            pl.BlockSpec((D_MODEL, TN_IN), lambda i, j: (0, j)),
            pl.BlockSpec((D_MODEL, LANES), lambda i, j: (0, 0)),
        ],
        out_specs=[
            pl.BlockSpec((TM_IN, TN_IN), lambda i, j: (i, j)),
            pl.BlockSpec((TM_IN, LANES), lambda i, j: (i, 0)),
        ],
        out_shape=[jax.ShapeDtypeStruct((M_ROWS, n), BF16),
                   jax.ShapeDtypeStruct((M_ROWS, LANES), F32)],
        scratch_shapes=[pltpu.VMEM((TM_IN, D_MODEL), BF16)],
        compiler_params=_cparams(("parallel", "arbitrary")),
    )(x2, norm_g, mod, mod, w_bf, wg_bf)


def _outproj_kernel(o_ref, w_ref, x_ref, gate_ref, fg_ref, out_ref, *, final_norm):
    def chunk_dot(r):
        return jnp.dot(o_ref[r * RC_OUT:(r + 1) * RC_OUT, :], w_ref[...],
                       preferred_element_type=F32)

    def chunk_store(r, y):
        rows = slice(r * RC_OUT, (r + 1) * RC_OUT)
        x = x_ref[rows, :] + gate_ref[...] * y
        if final_norm:
            ms = jnp.mean(x * x, axis=-1, keepdims=True)
            x = x * lax.rsqrt(ms + EPS) * fg_ref[...]
        out_ref[rows, :] = x

    _pipeline_row_chunks(TM_OUT // RC_OUT, chunk_dot, chunk_store)


def _outproj(o, w_bf, x2, mod, final_g, layer, final_norm):
    tiles_per_batch = SEQ // TM_OUT
    kern = functools.partial(_outproj_kernel, final_norm=final_norm)
    return pl.pallas_call(
        kern,
        grid=(M_ROWS // TM_OUT,),
        in_specs=[
            pl.BlockSpec((TM_OUT, D_INNER), lambda i: (i, 0)),
            pl.BlockSpec((D_INNER, D_MODEL), lambda i: (0, 0)),
            pl.BlockSpec((TM_OUT, D_MODEL), lambda i: (i, 0)),
            pl.BlockSpec((None, 1, D_MODEL),
                         lambda i: ((layer * ADA_ROWS + i // tiles_per_batch) * 3 + 2, 0, 0)),
            pl.BlockSpec((1, D_MODEL), lambda i: (0, 0)),
        ],
        out_specs=pl.BlockSpec((TM_OUT, D_MODEL), lambda i: (i, 0)),
        out_shape=jax.ShapeDtypeStruct((M_ROWS, D_MODEL), F32),
        compiler_params=_cparams(("parallel",)),
    )(o, w_bf, x2, mod, final_g)


def _aligned(start, multiple):
    return start if isinstance(start, int) else pl.multiple_of(start, multiple)


def _head_lanes(hh):
    return slice(hh * LANES, (hh + 1) * LANES)


def _diff_attn_tile(qi, *, finish_previous, accs, lam, q_ref, k_ref, vt_scr, m_scr, p_scr,
                    alpha_scr, fin_scr):
    heads = range(len(accs))
    lane = lax.broadcasted_iota(jnp.int32, (TQ, LANES), 1)
    is_a = (lane % DA_HEAD_DIM) < ROPE_HALF
    qs = []
    for hh in heads:
        q = q_ref[pl.ds(_aligned(qi * TQ, TQ), TQ), _head_lanes(hh)]
        zero = jnp.zeros_like(q)
        qs.append(jnp.concatenate([jnp.where(is_a, q, zero), jnp.where(is_a, zero, q)],
                                  axis=0))
        accs[hh][...] = jnp.zeros(accs[hh].shape, F32)
    m_scr[...] = jnp.full(m_scr.shape, NEG, F32)

    def deferred_pv(hh, prev_start, cs):
        vt = vt_scr[hh, :, pl.ds(_aligned(prev_start, TK), TK)]
        accs[hh][:, cs] = alpha_scr[hh, :, cs] * accs[hh][:, cs] + jnp.dot(
            vt, p_scr[hh, :, cs], preferred_element_type=F32)

    def block_masked_out(diag_offset, c):
        return diag_offset is not None and diag_offset > (c * CB) % TQ + CB - 1

    def key_round(start, diag_offset, prev_start, prev_diag_offset=None):
        ks = [k_ref[pl.ds(_aligned(start, TK), TK), _head_lanes(hh)] for hh in heads]
        for c in range(2 * TQ // CB):
            cs = slice(c * CB, (c + 1) * CB)
            q_lo = (c * CB) % TQ
            masked_out = block_masked_out(diag_offset, c)
            needs_mask = (diag_offset is not None and not masked_out
                          and diag_offset + TK - 1 > q_lo)
            for hh in heads:
                if not masked_out:
                    st = lax.dot_general(ks[hh], qs[hh][cs], (((1,), (1,)), ((), ())),
                                         preferred_element_type=F32)
                if prev_start is not None and not block_masked_out(prev_diag_offset, c):
                    deferred_pv(hh, prev_start, cs)
                if masked_out:
                    alpha_scr[hh, :, cs] = jnp.ones((1, CB), F32)
                    p_scr[hh, :, cs] = jnp.zeros((TK, CB), BF16)
                    continue
                if needs_mask:
                    kpos = lax.broadcasted_iota(jnp.int32, st.shape, 0) + diag_offset
                    qpos = lax.broadcasted_iota(jnp.int32, st.shape, 1) + q_lo
                    st = jnp.where(kpos <= qpos, st, NEG)
                sb = st.astype(BF16)
                m_prev = m_scr[hh, :, cs]
                m_new = jnp.maximum(m_prev, jnp.max(sb, axis=0, keepdims=True).astype(F32))
                alpha_scr[hh, :, cs] = jnp.exp2(m_prev - m_new)
                p_scr[hh, :, cs] = jnp.exp2(sb - m_new.astype(BF16))
                m_scr[hh, :, cs] = m_new

    diag = qi * TQ
    order = list(range(NSUB - 1, -1, -1))
    key_round(diag + order[0] * TK, order[0] * TK, None)
    for prev, u in zip(order[:-1], order[1:]):
        key_round(diag + u * TK, u * TK, diag + prev * TK, prev * TK)
    diag_last = diag + order[-1] * TK

    def full_tiles(first_tile, n_tiles):
        base = first_tile * TQ
        key_round(base, None, jnp.where(first_tile == 0, diag_last, base - TK))
        for u in range(1, n_tiles * NSUB):
            key_round(base + u * TK, None, base + (u - 1) * TK)

    def pair_body(kp, carry):
        full_tiles(2 * kp, 2)
        return carry

    finish_previous()
    if not (isinstance(qi, int) and qi == 0):
        lax.fori_loop(0, qi // 2, pair_body, 0)

        @pl.when(qi % 2 == 1)
        def _():
            full_tiles(qi - 1, 1)
    last_start = diag_last if isinstance(qi, int) and qi == 0 else diag - TK
    for c in range(2 * TQ // CB):
        for hh in heads:
            deferred_pv(hh, last_start, slice(c * CB, (c + 1) * CB))
    for hh in heads:
        a = accs[hh][...]
        on = a[0:DA_V_DIM, :] / a[DA_V_DIM:DA_V_DIM + 1, :]
        fin_scr[hh] = on[:, :TQ] - lam * on[:, TQ:]


def _diff_attn_kernel(q_ref, k_ref, v_ref, z_ref, lamv_ref, subw_ref, o_ref,
                      vt_scr, m_scr, acc_scr, p_scr, alpha_scr, fin_scr, *, lambda_init):
    nq = SEQ // TQ
    heads = range(HEADS_PER_STEP)
    for hh in heads:
        vt_scr[hh, 0:DA_V_DIM, :] = v_ref[:, _head_lanes(hh)].astype(F32).T.astype(BF16)
        vt_scr[hh, DA_V_DIM:, :] = jnp.ones((ONES_ROWS, SEQ), BF16)
    lamv = lamv_ref[...]
    lam = (jnp.exp(jnp.sum(lamv[0:1, :] * lamv[1:2, :], keepdims=True))
           - jnp.exp(jnp.sum(lamv[2:3, :] * lamv[3:4, :], keepdims=True)) + lambda_init)
    subw = subw_ref[...] * (1.0 - lambda_init)

    def epilogue(qi):
        rows = pl.ds(_aligned(qi * TQ, TQ), TQ)
        for hh in heads:
            ot = fin_scr[hh]
            ms = jnp.mean(ot * ot, axis=0, keepdims=True)
            ot = ot * lax.rsqrt(ms + EPS) * subw
            z = z_ref[rows, _head_lanes(hh)].astype(F32)
            o_ref[rows, _head_lanes(hh)] = (ot.T * (z * jax.nn.sigmoid(z))).astype(BF16)

    tile = functools.partial(_diff_attn_tile, accs=[acc_scr.at[hh] for hh in heads], lam=lam,
                             q_ref=q_ref, k_ref=k_ref, vt_scr=vt_scr, m_scr=m_scr,
                             p_scr=p_scr, alpha_scr=alpha_scr, fin_scr=fin_scr)
    tile(0, finish_previous=lambda: None)

    def tile_body(qi, carry):
        tile(qi, finish_previous=lambda: epilogue(qi - 1))
        return carry

    lax.fori_loop(1, nq, tile_body, 0)
    epilogue(nq - 1)


def _diff_attn(qkvz, lamv, subw_col, lambda_init):
    kern = functools.partial(_diff_attn_kernel, lambda_init=lambda_init)
    width = HEADS_PER_STEP * LANES
    head_block = lambda which: pl.BlockSpec((None, SEQ, width), lambda b, h: (which, b, h))
    acc_rows = DA_V_DIM + ONES_ROWS
    return pl.pallas_call(
        kern,
        grid=(BATCH, DA_HEADS // HEADS_PER_STEP),
        in_specs=[
            head_block(0), head_block(1), head_block(2), head_block(3),
            pl.BlockSpec((8, LANES), lambda b, h: (0, 0)),
            pl.BlockSpec((DA_V_DIM, 1), lambda b, h: (0, 0)),
        ],
        out_specs=pl.BlockSpec((SEQ, width), lambda b, h: (b, h)),
        out_shape=jax.ShapeDtypeStruct((M_ROWS, D_INNER), BF16),
        scratch_shapes=[
            pltpu.VMEM((HEADS_PER_STEP, acc_rows, SEQ), BF16),
            pltpu.VMEM((HEADS_PER_STEP, 1, 2 * TQ), F32),
            pltpu.VMEM((HEADS_PER_STEP, acc_rows, 2 * TQ), F32),
            pltpu.VMEM((HEADS_PER_STEP, TK, 2 * TQ), BF16),
            pltpu.VMEM((HEADS_PER_STEP, 1, 2 * TQ), F32),
            pltpu.VMEM((HEADS_PER_STEP, DA_V_DIM, TQ), F32),
        ],
        compiler_params=_cparams(("parallel", "parallel")),
    )(qkvz, qkvz, qkvz, qkvz, lamv, subw_col)


def _gla_kernel(q_ref, k_ref, v_ref, z_ref, g_ref, wg_ref, bg_ref, tri_ref, nw_ref,
                o_ref, st_scr):
    t = pl.program_id(1)

    @pl.when(t == 0)
    def _():
        st_scr[...] = jnp.zeros(st_scr.shape, F32)

    nc = GLA_ROWS // GLA_CHUNK
    tri = tri_ref[...]
    bcums = []
    for bb in range(GLA_BATCH_PER_STEP):
        pre = (jnp.dot(g_ref[bb].astype(BF16), wg_ref[...], preferred_element_type=F32)
               + bg_ref[...])
        log_a = ((jnp.minimum(pre, 0.0) - jnp.log(1.0 + jnp.exp(-jnp.abs(pre))))
                 * (1.0 / GLA_GATE_TAU))
        hi = log_a.astype(BF16)
        lo = (log_a - hi.astype(F32)).astype(BF16)
        bcums.append(jnp.dot(tri, hi, preferred_element_type=F32)
                     + jnp.dot(tri, lo, preferred_element_type=F32))

    row = lax.broadcasted_iota(jnp.int32, (GLA_ROWS, GLA_ROWS), 0)
    col = lax.broadcasted_iota(jnp.int32, (GLA_ROWS, GLA_ROWS), 1)
    causal = col <= row
    nw = nw_ref[...]
    nt_dims = (((1,), (1,)), ((), ()))

    for h, bb in [(h, bb) for h in range(GLA_HEADS) for bb in range(GLA_BATCH_PER_STEP)]:
        ks = slice(h * GLA_HEAD_K, (h + 1) * GLA_HEAD_K)
        vs = slice(h * GLA_HEAD_V, (h + 1) * GLA_HEAD_V)
        bh = bcums[bb][:, ks]
        b_end = [bh[(i + 1) * GLA_CHUNK - 1:(i + 1) * GLA_CHUNK, :] for i in range(nc)]
        def row_factor(e):
            return jnp.broadcast_to(jnp.exp(e), (GLA_CHUNK, GLA_HEAD_K)).astype(BF16)

        q_dec, q_int, k_inv, k_end, k_st = [], [], [], [], []
        for i in range(nc):
            rows = slice(i * GLA_CHUNK, (i + 1) * GLA_CHUNK)
            b = bh[rows, :] if i == 0 else bh[rows, :] - b_end[i - 1]
            qd = q_ref[bb, rows, ks] * (jnp.exp(b) * (GLA_HEAD_K ** -0.5)).astype(BF16)
            ki = k_ref[bb, rows, ks] * jnp.exp(-b).astype(BF16)
            ke = ki * row_factor(b_end[i] if i == 0 else b_end[i] - b_end[i - 1])
            q_dec.append(qd)
            q_int.append(qd if i == 0 else qd * row_factor(b_end[i - 1]))
            k_inv.append(ki)
            k_end.append(ke)
            k_st.append(ke if i == nc - 1 else ke * row_factor(b_end[nc - 1] - b_end[i]))
        attn_rows = []
        for i in range(nc):
            pieces = []
            for j in range(i):
                pieces.append(k_end[j] if j == i - 1
                              else k_end[j] * row_factor(b_end[i - 1] - b_end[j]))
            pieces.append(k_inv[i])
            if i < nc - 1:
                pieces.append(jnp.zeros(((nc - 1 - i) * GLA_CHUNK, GLA_HEAD_K), BF16))
            attn_rows.append(lax.dot_general(q_dec[i], jnp.concatenate(pieces, axis=0), nt_dims,
                                             preferred_element_type=F32))
        attn = jnp.where(causal, jnp.concatenate(attn_rows, axis=0), 0.0).astype(BF16)
        st = st_scr[bb, h]
        v = v_ref[bb, :, vs]
        o = jnp.dot(jnp.concatenate([attn] + [jnp.concatenate(q_int, axis=0)], axis=1),
                    jnp.concatenate([v, st.astype(BF16)], axis=0),
                    preferred_element_type=F32)
        dec_col = jnp.broadcast_to(jnp.exp(b_end[nc - 1]), (LANES, GLA_HEAD_K)).T
        dec = jnp.concatenate([dec_col] * (GLA_HEAD_V // LANES), axis=1)
        st_scr[bb, h] = st * dec + lax.dot_general(
            jnp.concatenate(k_st, axis=0), v, (((0,), (0,)), ((), ())),
            preferred_element_type=F32)
        ms = jnp.mean(o * o, axis=-1, keepdims=True)
        z = z_ref[bb, :, vs]
        gate = (z * jax.nn.sigmoid(z)).astype(F32)
        o_ref[bb, :, vs] = (o * lax.rsqrt(ms + EPS) * nw * gate).astype(BF16)


def _gla(main, glow, wg_bf, bg, tri, nw):
    nt = SEQ // GLA_ROWS
    bps = GLA_BATCH_PER_STEP
    assert 2 * GLA_DK == D_INNER and BATCH % bps == 0
    main3 = main.reshape(BATCH, SEQ, main.shape[1])
    glow3 = glow.reshape(BATCH, SEQ, LANES)
    out = pl.pallas_call(
        _gla_kernel,
        grid=(BATCH // bps, nt),
        in_specs=[
            pl.BlockSpec((bps, GLA_ROWS, GLA_DK), lambda b, t: (b, t, 0)),
            pl.BlockSpec((bps, GLA_ROWS, GLA_DK), lambda b, t: (b, t, 1)),
            pl.BlockSpec((bps, GLA_ROWS, D_INNER), lambda b, t: (b, t, 1)),
            pl.BlockSpec((bps, GLA_ROWS, D_INNER), lambda b, t: (b, t, 2)),
            pl.BlockSpec((bps, GLA_ROWS, LANES), lambda b, t: (b, t, 0)),
            pl.BlockSpec((LANES, GLA_DK), lambda b, t: (0, 0)),
            pl.BlockSpec((1, GLA_DK), lambda b, t: (0, 0)),
            pl.BlockSpec((GLA_ROWS, GLA_ROWS), lambda b, t: (0, 0)),
            pl.BlockSpec((1, GLA_HEAD_V), lambda b, t: (0, 0)),
        ],
        out_specs=pl.BlockSpec((bps, GLA_ROWS, D_INNER), lambda b, t: (b, t, 0)),
        out_shape=jax.ShapeDtypeStruct((BATCH, SEQ, D_INNER), BF16),
        scratch_shapes=[pltpu.VMEM((bps, GLA_HEADS, GLA_HEAD_K, GLA_HEAD_V), F32)],
        compiler_params=_cparams(("parallel", "arbitrary")),
    )(main3, main3, main3, main3, glow3, wg_bf, bg, tri, nw)
    return out.reshape(M_ROWS, D_INNER)


def _prep_da_w_kernel(w_ref, o_ref, *, n_qk_tiles):
    j = pl.program_id(0)

    @pl.when(j < n_qk_tiles)
    def _():
        lane = lax.broadcasted_iota(jnp.int32, (w_ref.shape[0], LANES), 1)
        take_next = (lane >= ROPE_HALF) & (lane < 2 * ROPE_HALF)
        take_prev = (lane >= 2 * ROPE_HALF) & (lane < 3 * ROPE_HALF)
        for gidx in range(w_ref.shape[1] // LANES):
            cols = slice(gidx * LANES, (gidx + 1) * LANES)
            w = w_ref[:, cols]
            nxt = pltpu.roll(w, LANES - ROPE_HALF, 1)
            prv = pltpu.roll(w, ROPE_HALF, 1)
            o_ref[:, cols] = jnp.where(take_next, nxt,
                                       jnp.where(take_prev, prv, w)).astype(BF16)

    @pl.when(j >= n_qk_tiles)
    def _():
        o_ref[...] = w_ref[...].astype(BF16)


def _prep_da_w(w_in):
    d, n = w_in.shape
    kern = functools.partial(_prep_da_w_kernel, n_qk_tiles=2 * D_INNER // TN_IN)
    return pl.pallas_call(
        kern,
        grid=(n // TN_IN,),
        in_specs=[pl.BlockSpec((d, TN_IN), lambda j: (0, j))],
        out_specs=pl.BlockSpec((d, TN_IN), lambda j: (0, j)),
        out_shape=jax.ShapeDtypeStruct((d, n), BF16),
        compiler_params=_cparams(("parallel",)),
    )(w_in)


def kernel(x, c, positions, ada_w, ada_b, norm_g, da_w_in, da_lam_q1, da_lam_k1, da_lam_q2,
           da_lam_k2, da_subln_w, da_w_out, gla_w_in, gla_w_gate_up, gla_b_gate, gla_norm_w,
           gla_w_out, final_g):
    x2 = x.reshape(M_ROWS, D_MODEL)
    c_pad = jnp.pad(c, ((0, ADA_ROWS - BATCH), (0, 0))).astype(BF16)
    mod = _ada(c_pad, ada_w, ada_b.reshape(DEPTH, 1, 3 * D_MODEL))
    mod = mod.reshape(DEPTH * ADA_ROWS * 3, 1, D_MODEL)
    norm_g3 = norm_g.reshape(DEPTH, 1, D_MODEL)

    inv_freq = ROPE_THETA ** (-jnp.arange(0, DA_HEAD_DIM, 2, dtype=F32) / DA_HEAD_DIM)
    cos, sin = _rope_tables(positions.reshape(BATCH, 1, SEQ), inv_freq.reshape(ROPE_HALF, 1))

    lambda_init = 0.8 - 0.6 * math.exp(-0.3 * 0)
    qkvz = _inproj_da(x2, norm_g3, mod, _prep_da_w(da_w_in[0]), cos, sin, layer=0)
    lamv = jnp.concatenate([da_lam_q1, da_lam_k1, da_lam_q2, da_lam_k2], axis=0)
    lamv = jnp.pad(lamv, ((0, 4), (0, LANES - DA_HEAD_DIM)))
    o = _diff_attn(qkvz, lamv, da_subln_w[0].reshape(DA_V_DIM, 1), lambda_init)
    x2 = _outproj(o, da_w_out[0].astype(BF16), x2, mod, final_g.reshape(1, D_MODEL),
                  layer=0, final_norm=False)

    w = gla_w_in[0]
    wg_in = jnp.pad(w[:, GLA_MAIN:], ((0, 0), (0, LANES - GLA_GATE_RANK))).astype(BF16)
    main, glow = _inproj_gla(x2, norm_g3, mod, w.astype(BF16), wg_in, layer=1)
    wg_up = jnp.pad(gla_w_gate_up[0], ((0, LANES - GLA_GATE_RANK), (0, 0))).astype(BF16)
    r = jnp.arange(GLA_ROWS)
    tri = (r[:, None] >= r[None, :]).astype(BF16)
    o = _gla(main, glow, wg_up, gla_b_gate, tri, gla_norm_w)
    out = _outproj(o, gla_w_out[0].astype(BF16), x2, mod, final_g.reshape(1, D_MODEL),
                   layer=1, final_norm=True)
    return out.reshape(BATCH, SEQ, D_MODEL)
```

```python
import functools
import math

import jax
import jax.numpy as jnp
from jax import lax
from jax.experimental import pallas as pl
from jax.experimental.pallas import tpu as pltpu

F32 = jnp.float32
BF16 = jnp.bfloat16

D_MODEL = 1024
BATCH = 4
SEQ = 4096
DEPTH = 2
D_INNER = 2 * D_MODEL
EPS = 1e-6
M_ROWS = BATCH * SEQ

DA_HEADS = 16
DA_HEAD_DIM = 64
DA_V_DIM = 128
ROPE_THETA = 10000.0
ROPE_HALF = DA_HEAD_DIM // 2

GLA_HEADS = 4
GLA_DK = 1024
GLA_HEAD_K = 256
GLA_HEAD_V = 512
GLA_GATE_RANK = 16
GLA_GATE_TAU = 16.0
GLA_CHUNK = 64
GLA_MAIN = 2 * GLA_DK + 2 * D_INNER

LANES = 128
SUBLANES = 8
ADA_ROWS = 16
VMEM_LIMIT = 56 * 1024 * 1024

TM_IN, TN_IN = 2048, 1024
RC_IN = 256
RC_OUT = 256
TM_OUT = 1024
TQ = 1024
TK = 256
NSUB = TQ // TK
CB = 256
assert TQ % TK == 0 and TQ % CB == 0
HEADS_PER_STEP = 2
ONES_ROWS = 16
GLA_ROWS = 256
GLA_BATCH_PER_STEP = 2
TM_ROPE = 2048
TN_ADA = 3 * D_MODEL

NEG = -1e30


def _cparams(sem):
    return pltpu.CompilerParams(dimension_semantics=sem, vmem_limit_bytes=VMEM_LIMIT)


def _ada_kernel(c_ref, w_ref, b_ref, o_ref):
    o_ref[...] = jnp.dot(c_ref[...], w_ref[...].astype(BF16),
                         preferred_element_type=F32) + b_ref[...]


def _ada(c_pad, ada_w, ada_b3):
    n = 3 * D_MODEL
    return pl.pallas_call(
        _ada_kernel,
        grid=(DEPTH, n // TN_ADA),
        in_specs=[
            pl.BlockSpec((ADA_ROWS, D_MODEL), lambda l, j: (0, 0)),
            pl.BlockSpec((None, D_MODEL, TN_ADA), lambda l, j: (l, 0, j)),
            pl.BlockSpec((None, 1, TN_ADA), lambda l, j: (l, 0, j)),
        ],
        out_specs=pl.BlockSpec((None, ADA_ROWS, TN_ADA), lambda l, j: (l, 0, j)),
        out_shape=jax.ShapeDtypeStruct((DEPTH, ADA_ROWS, n), F32),
        compiler_params=_cparams(("parallel", "parallel")),
    )(c_pad, ada_w, ada_b3)


def _rope_kernel(pos_ref, invf_ref, cos_ref, sin_ref):
    ang = invf_ref[...] * pos_ref[...].astype(F32)
    c = jnp.cos(ang)
    s = jnp.sin(ang)
    cos_ref[...] = jnp.concatenate([c, c, c, c], axis=0).T
    sin_ref[...] = jnp.concatenate([-s, -s, s, s], axis=0).T


def _rope_tables(pos3, invf):
    nt = SEQ // TM_ROPE
    return pl.pallas_call(
        _rope_kernel,
        grid=(BATCH, nt),
        in_specs=[
            pl.BlockSpec((None, 1, TM_ROPE), lambda b, t: (b, 0, t)),
            pl.BlockSpec((ROPE_HALF, 1), lambda b, t: (0, 0)),
        ],
        out_specs=[
            pl.BlockSpec((TM_ROPE, LANES), lambda b, t: (b * nt + t, 0)),
            pl.BlockSpec((TM_ROPE, LANES), lambda b, t: (b * nt + t, 0)),
        ],
        out_shape=[jax.ShapeDtypeStruct((M_ROWS, LANES), F32)] * 2,
        compiler_params=_cparams(("parallel", "parallel")),
    )(pos3, invf)


def _pipeline_row_chunks(n_chunks, chunk_dot, chunk_store, chunk_prepare=None):
    if chunk_prepare is not None:
        chunk_prepare(0)
    acc = chunk_dot(0)
    for r in range(n_chunks):
        nxt = None
        if r + 1 < n_chunks:
            if chunk_prepare is not None:
                chunk_prepare(r + 1)
            nxt = chunk_dot(r + 1)
        chunk_store(r, acc)
        acc = nxt


def _norm_modulate(x, g, scale, shift):
    ms = jnp.mean(x * x, axis=-1, keepdims=True)
    return x * lax.rsqrt(ms + EPS) * (g * (1.0 + scale)) + shift


def _inproj_da_kernel(x_ref, g_ref, scale_ref, shift_ref, w_ref, cos_ref, sin_ref,
                      o_ref, h_scr, *, n_q_tiles, n_rope_tiles, q_scale):
    j = pl.program_id(1)

    @pl.when(j == 0)
    def _():
        for r in range(TM_IN // RC_IN):
            rows = slice(r * RC_IN, (r + 1) * RC_IN)
            h_scr[rows, :] = _norm_modulate(x_ref[rows, :], g_ref[...], scale_ref[...],
                                            shift_ref[...]).astype(BF16)

    is_rope = j < n_rope_tiles
    sc = jnp.where(j < n_q_tiles, q_scale, 1.0).astype(F32)
    cos = jnp.where(is_rope, cos_ref[...] * sc, 1.0)
    sin = jnp.where(is_rope, sin_ref[...] * sc, 0.0)
    n_chunks = TM_IN // RC_IN

    def chunk_dot(r):
        return jnp.dot(h_scr[r * RC_IN:(r + 1) * RC_IN, :], w_ref[...],
                       preferred_element_type=F32)

    def chunk_store(r, acc):
        rows = slice(r * RC_IN, (r + 1) * RC_IN)
        for gidx in range(acc.shape[1] // LANES):
            cols = slice(gidx * LANES, (gidx + 1) * LANES)
            t = acc[:, cols]
            o_ref[rows, cols] = (t * cos[rows, :]
                                 + pltpu.roll(t, LANES // 2, 1) * sin[rows, :]).astype(BF16)

    _pipeline_row_chunks(n_chunks, chunk_dot, chunk_store)


def _inproj_da(x2, norm_g, mod, w_bf, cos, sin, layer):
    n = w_bf.shape[1]
    seg_tiles = D_INNER // TN_IN
    tiles_per_batch = SEQ // TM_IN

    def mod_idx(which):
        return lambda i, j: ((layer * ADA_ROWS + i // tiles_per_batch) * 3 + which, 0, 0)

    kern = functools.partial(_inproj_da_kernel, n_q_tiles=seg_tiles,
                             n_rope_tiles=2 * seg_tiles,
                             q_scale=DA_HEAD_DIM ** -0.5 * math.log2(math.e))
    return pl.pallas_call(
        kern,
        grid=(M_ROWS // TM_IN, n // TN_IN),
        in_specs=[
            pl.BlockSpec((TM_IN, D_MODEL), lambda i, j: (i, 0)),
            pl.BlockSpec((None, 1, D_MODEL), lambda i, j: (layer, 0, 0)),
            pl.BlockSpec((None, 1, D_MODEL), mod_idx(1)),
            pl.BlockSpec((None, 1, D_MODEL), mod_idx(0)),
            pl.BlockSpec((D_MODEL, TN_IN), lambda i, j: (0, j)),
            pl.BlockSpec((TM_IN, LANES), lambda i, j: (i, 0)),
            pl.BlockSpec((TM_IN, LANES), lambda i, j: (i, 0)),
        ],
        out_specs=pl.BlockSpec((None, TM_IN, TN_IN),
                               lambda i, j: (j // seg_tiles, i, j % seg_tiles)),
        out_shape=jax.ShapeDtypeStruct((4, M_ROWS, D_INNER), BF16),
        scratch_shapes=[pltpu.VMEM((TM_IN, D_MODEL), BF16)],
        compiler_params=_cparams(("parallel", "arbitrary")),
    )(x2, norm_g, mod, mod, w_bf, cos, sin)


def _inproj_gla_kernel(x_ref, g_ref, scale_ref, shift_ref, w_ref, wg_ref,
                       o_ref, glow_ref, h_scr):
    j = pl.program_id(1)
    n_chunks = TM_IN // RC_IN

    def chunk_norm(r):
        rows = slice(r * RC_IN, (r + 1) * RC_IN)
        h = _norm_modulate(x_ref[rows, :], g_ref[...], scale_ref[...],
                           shift_ref[...]).astype(BF16)
        h_scr[rows, :] = h
        glow_ref[rows, :] = jnp.dot(h, wg_ref[...], preferred_element_type=F32)

    def chunk_dot(r):
        return jnp.dot(h_scr[r * RC_IN:(r + 1) * RC_IN, :], w_ref[...],
                       preferred_element_type=F32)

    def chunk_store(r, acc):
        o_ref[r * RC_IN:(r + 1) * RC_IN, :] = acc.astype(BF16)

    @pl.when(j == 0)
    def _():
        _pipeline_row_chunks(n_chunks, chunk_dot, chunk_store, chunk_norm)

    @pl.when(j > 0)
    def _():
        _pipeline_row_chunks(n_chunks, chunk_dot, chunk_store)


def _inproj_gla(x2, norm_g, mod, w_bf, wg_bf, layer):
    n = GLA_MAIN
    tiles_per_batch = SEQ // TM_IN

    def mod_idx(which):
        return lambda i, j: ((layer * ADA_ROWS + i // tiles_per_batch) * 3 + which, 0, 0)

    return pl.pallas_call(
        _inproj_gla_kernel,
        grid=(M_ROWS // TM_IN, n // TN_IN),
        in_specs=[
            pl.BlockSpec((TM_IN, D_MODEL), lambda i, j: (i, 0)),
            pl.BlockSpec((None, 1, D_MODEL), lambda i, j: (layer, 0, 0)),
            pl.BlockSpec((None, 1, D_MODEL), mod_idx(1)),
            pl.BlockSpec((None, 1, D_MODEL), mod_idx(0)),
            pl.BlockSpec((D_MODEL, TN_IN), lambda i, j: (0, j)),
            pl.BlockSpec((D_MODEL, LANES), lambda i, j: (0, 0)),
        ],
        out_specs=[
            pl.BlockSpec((TM_IN, TN_IN), lambda i, j: (i, j)),
            pl.BlockSpec((TM_IN, LANES), lambda i, j: (i, 0)),
        ],
        out_shape=[jax.ShapeDtypeStruct((M_ROWS, n), BF16),
                   jax.ShapeDtypeStruct((M_ROWS, LANES), F32)],
        scratch_shapes=[pltpu.VMEM((TM_IN, D_MODEL), BF16)],
        compiler_params=_cparams(("parallel", "arbitrary")),
    )(x2, norm_g, mod, mod, w_bf, wg_bf)


def _outproj_kernel(o_ref, w_ref, x_ref, gate_ref, fg_ref, out_ref, *, final_norm):
    def chunk_dot(r):
        return jnp.dot(o_ref[r * RC_OUT:(r + 1) * RC_OUT, :], w_ref[...],
                       preferred_element_type=F32)

    def chunk_store(r, y):
        rows = slice(r * RC_OUT, (r + 1) * RC_OUT)
        x = x_ref[rows, :] + gate_ref[...] * y
        if final_norm:
            ms = jnp.mean(x * x, axis=-1, keepdims=True)
            x = x * lax.rsqrt(ms + EPS) * fg_ref[...]
        out_ref[rows, :] = x

    _pipeline_row_chunks(TM_OUT // RC_OUT, chunk_dot, chunk_store)


def _outproj(o, w_bf, x2, mod, final_g, layer, final_norm):
    tiles_per_batch = SEQ // TM_OUT
    kern = functools.partial(_outproj_kernel, final_norm=final_norm)
    return pl.pallas_call(
        kern,
        grid=(M_ROWS // TM_OUT,),
        in_specs=[
            pl.BlockSpec((TM_OUT, D_INNER), lambda i: (i, 0)),
            pl.BlockSpec((D_INNER, D_MODEL), lambda i: (0, 0)),
            pl.BlockSpec((TM_OUT, D_MODEL), lambda i: (i, 0)),
            pl.BlockSpec((None, 1, D_MODEL),
                         lambda i: ((layer * ADA_ROWS + i // tiles_per_batch) * 3 + 2, 0, 0)),
            pl.BlockSpec((1, D_MODEL), lambda i: (0, 0)),
        ],
        out_specs=pl.BlockSpec((TM_OUT, D_MODEL), lambda i: (i, 0)),
        out_shape=jax.ShapeDtypeStruct((M_ROWS, D_MODEL), F32),
        compiler_params=_cparams(("parallel",)),
    )(o, w_bf, x2, mod, final_g)


def _aligned(start, multiple):
    return start if isinstance(start, int) else pl.multiple_of(start, multiple)


def _head_lanes(hh):
    return slice(hh * LANES, (hh + 1) * LANES)


def _diff_attn_tile(qi, *, finish_previous, accs, lam, q_ref, k_ref, vt_scr, m_scr, p_scr,
                    alpha_scr, fin_scr):
    heads = range(len(accs))
    lane = lax.broadcasted_iota(jnp.int32, (TQ, LANES), 1)
    is_a = (lane % DA_HEAD_DIM) < ROPE_HALF
    qs = []
    for hh in heads:
        q = q_ref[pl.ds(_aligned(qi * TQ, TQ), TQ), _head_lanes(hh)]
        zero = jnp.zeros_like(q)
        qs.append(jnp.concatenate([jnp.where(is_a, q, zero), jnp.where(is_a, zero, q)],
                                  axis=0))
        accs[hh][...] = jnp.zeros(accs[hh].shape, F32)
    m_scr[...] = jnp.full(m_scr.shape, NEG, F32)

    def deferred_pv(hh, prev_start, cs):
        vt = vt_scr[hh, :, pl.ds(_aligned(prev_start, TK), TK)]
        accs[hh][:, cs] = alpha_scr[hh, :, cs] * accs[hh][:, cs] + jnp.dot(
            vt, p_scr[hh, :, cs], preferred_element_type=F32)

    def block_masked_out(diag_offset, c):
        return diag_offset is not None and diag_offset > (c * CB) % TQ + CB - 1

    def key_round(start, diag_offset, prev_start, prev_diag_offset=None):
        ks = [k_ref[pl.ds(_aligned(start, TK), TK), _head_lanes(hh)] for hh in heads]
        for c in range(2 * TQ // CB):
            cs = slice(c * CB, (c + 1) * CB)
            q_lo = (c * CB) % TQ
            masked_out = block_masked_out(diag_offset, c)
            needs_mask = (diag_offset is not None and not masked_out
                          and diag_offset + TK - 1 > q_lo)
            for hh in heads:
                if not masked_out:
                    st = lax.dot_general(ks[hh], qs[hh][cs], (((1,), (1,)), ((), ())),
                                         preferred_element_type=F32)
                if prev_start is not None and not block_masked_out(prev_diag_offset, c):
                    deferred_pv(hh, prev_start, cs)
                if masked_out:
                    alpha_scr[hh, :, cs] = jnp.ones((1, CB), F32)
                    p_scr[hh, :, cs] = jnp.zeros((TK, CB), BF16)
                    continue
                if needs_mask:
                    kpos = lax.broadcasted_iota(jnp.int32, st.shape, 0) + diag_offset
                    qpos = lax.broadcasted_iota(jnp.int32, st.shape, 1) + q_lo
                    st = jnp.where(kpos <= qpos, st, NEG)
                sb = st.astype(BF16)
                m_prev = m_scr[hh, :, cs]
                m_new = jnp.maximum(m_prev, jnp.max(sb, axis=0, keepdims=True).astype(F32))
                alpha_scr[hh, :, cs] = jnp.exp2(m_prev - m_new)
                p_scr[hh, :, cs] = jnp.exp2(sb - m_new.astype(BF16))
                m_scr[hh, :, cs] = m_new

    diag = qi * TQ
    order = list(range(NSUB - 1, -1, -1))
    key_round(diag + order[0] * TK, order[0] * TK, None)
    for prev, u in zip(order[:-1], order[1:]):
        key_round(diag + u * TK, u * TK, diag + prev * TK, prev * TK)
    diag_last = diag + order[-1] * TK

    def full_tiles(first_tile, n_tiles):
        base = first_tile * TQ
        key_round(base, None, jnp.where(first_tile == 0, diag_last, base - TK))
        for u in range(1, n_tiles * NSUB):
            key_round(base + u * TK, None, base + (u - 1) * TK)

    def pair_body(kp, carry):
        full_tiles(2 * kp, 2)
        return carry

    finish_previous()
    if not (isinstance(qi, int) and qi == 0):
        lax.fori_loop(0, qi // 2, pair_body, 0)

        @pl.when(qi % 2 == 1)
        def _():
            full_tiles(qi - 1, 1)
    last_start = diag_last if isinstance(qi, int) and qi == 0 else diag - TK
    for c in range(2 * TQ // CB):
        for hh in heads:
            deferred_pv(hh, last_start, slice(c * CB, (c + 1) * CB))
    for hh in heads:
        a = accs[hh][...]
        on = a[0:DA_V_DIM, :] / a[DA_V_DIM:DA_V_DIM + 1, :]
        fin_scr[hh] = on[:, :TQ] - lam * on[:, TQ:]


def _diff_attn_kernel(q_ref, k_ref, v_ref, z_ref, lamv_ref, subw_ref, o_ref,
                      vt_scr, m_scr, acc_scr, p_scr, alpha_scr, fin_scr, *, lambda_init):
    nq = SEQ // TQ
    heads = range(HEADS_PER_STEP)
    for hh in heads:
        vt_scr[hh, 0:DA_V_DIM, :] = v_ref[:, _head_lanes(hh)].astype(F32).T.astype(BF16)
        vt_scr[hh, DA_V_DIM:, :] = jnp.ones((ONES_ROWS, SEQ), BF16)
    lamv = lamv_ref[...]
    lam = (jnp.exp(jnp.sum(lamv[0:1, :] * lamv[1:2, :], keepdims=True))
           - jnp.exp(jnp.sum(lamv[2:3, :] * lamv[3:4, :], keepdims=True)) + lambda_init)
    subw = subw_ref[...] * (1.0 - lambda_init)

    def epilogue(qi):
        rows = pl.ds(_aligned(qi * TQ, TQ), TQ)
        for hh in heads:
            ot = fin_scr[hh]
            ms = jnp.mean(ot * ot, axis=0, keepdims=True)
            ot = ot * lax.rsqrt(ms + EPS) * subw
            z = z_ref[rows, _head_lanes(hh)].astype(F32)
            o_ref[rows, _head_lanes(hh)] = (ot.T * (z * jax.nn.sigmoid(z))).astype(BF16)

    tile = functools.partial(_diff_attn_tile, accs=[acc_scr.at[hh] for hh in heads], lam=lam,
                             q_ref=q_ref, k_ref=k_ref, vt_scr=vt_scr, m_scr=m_scr,
                             p_scr=p_scr, alpha_scr=alpha_scr, fin_scr=fin_scr)
    tile(0, finish_previous=lambda: None)

    def tile_body(qi, carry):
        tile(qi, finish_previous=lambda: epilogue(qi - 1))
        return carry

    lax.fori_loop(1, nq, tile_body, 0)
    epilogue(nq - 1)


def _diff_attn(qkvz, lamv, subw_col, lambda_init):
    kern = functools.partial(_diff_attn_kernel, lambda_init=lambda_init)
    width = HEADS_PER_STEP * LANES
    head_block = lambda which: pl.BlockSpec((None, SEQ, width), lambda b, h: (which, b, h))
    acc_rows = DA_V_DIM + ONES_ROWS
    return pl.pallas_call(
        kern,
        grid=(BATCH, DA_HEADS // HEADS_PER_STEP),
        in_specs=[
            head_block(0), head_block(1), head_block(2), head_block(3),
            pl.BlockSpec((SUBLANES, LANES), lambda b, h: (0, 0)),
            pl.BlockSpec((DA_V_DIM, 1), lambda b, h: (0, 0)),
        ],
        out_specs=pl.BlockSpec((SEQ, width), lambda b, h: (b, h)),
        out_shape=jax.ShapeDtypeStruct((M_ROWS, D_INNER), BF16),
        scratch_shapes=[
            pltpu.VMEM((HEADS_PER_STEP, acc_rows, SEQ), BF16),
            pltpu.VMEM((HEADS_PER_STEP, 1, 2 * TQ), F32),
            pltpu.VMEM((HEADS_PER_STEP, acc_rows, 2 * TQ), F32),
            pltpu.VMEM((HEADS_PER_STEP, TK, 2 * TQ), BF16),
            pltpu.VMEM((HEADS_PER_STEP, 1, 2 * TQ), F32),
            pltpu.VMEM((HEADS_PER_STEP, DA_V_DIM, TQ), F32),
        ],
        compiler_params=_cparams(("parallel", "parallel")),
    )(qkvz, qkvz, qkvz, qkvz, lamv, subw_col)


def _gla_kernel(q_ref, k_ref, v_ref, z_ref, g_ref, wg_ref, bg_ref, tri_ref, nw_ref,
                o_ref, st_scr):
    t = pl.program_id(1)

    @pl.when(t == 0)
    def _():
        st_scr[...] = jnp.zeros(st_scr.shape, F32)

    nc = GLA_ROWS // GLA_CHUNK
    tri = tri_ref[...]
    bcums = []
    for bb in range(GLA_BATCH_PER_STEP):
        pre = (jnp.dot(g_ref[bb].astype(BF16), wg_ref[...], preferred_element_type=F32)
               + bg_ref[...])
        log_a = ((jnp.minimum(pre, 0.0) - jnp.log(1.0 + jnp.exp(-jnp.abs(pre))))
                 * (1.0 / GLA_GATE_TAU))
        hi = log_a.astype(BF16)
        lo = (log_a - hi.astype(F32)).astype(BF16)
        bcums.append(jnp.dot(tri, hi, preferred_element_type=F32)
                     + jnp.dot(tri, lo, preferred_element_type=F32))

    row = lax.broadcasted_iota(jnp.int32, (GLA_ROWS, GLA_ROWS), 0)
    col = lax.broadcasted_iota(jnp.int32, (GLA_ROWS, GLA_ROWS), 1)
    causal = col <= row
    nw = nw_ref[...]
    nt_dims = (((1,), (1,)), ((), ()))

    for h, bb in [(h, bb) for h in range(GLA_HEADS) for bb in range(GLA_BATCH_PER_STEP)]:
        ks = slice(h * GLA_HEAD_K, (h + 1) * GLA_HEAD_K)
        vs = slice(h * GLA_HEAD_V, (h + 1) * GLA_HEAD_V)
        bh = bcums[bb][:, ks]
        b_end = [bh[(i + 1) * GLA_CHUNK - 1:(i + 1) * GLA_CHUNK, :] for i in range(nc)]
        def row_factor(e):
            return jnp.broadcast_to(jnp.exp(e), (GLA_CHUNK, GLA_HEAD_K)).astype(BF16)

        q_dec, q_int, k_inv, k_end, k_st = [], [], [], [], []
        for i in range(nc):
            rows = slice(i * GLA_CHUNK, (i + 1) * GLA_CHUNK)
            b = bh[rows, :] if i == 0 else bh[rows, :] - b_end[i - 1]
            qd = q_ref[bb, rows, ks] * (jnp.exp(b) * (GLA_HEAD_K ** -0.5)).astype(BF16)
            ki = k_ref[bb, rows, ks] * jnp.exp(-b).astype(BF16)
            ke = ki * row_factor(b_end[i] if i == 0 else b_end[i] - b_end[i - 1])
            q_dec.append(qd)
            q_int.append(qd if i == 0 else qd * row_factor(b_end[i - 1]))
            k_inv.append(ki)
            k_end.append(ke)
            k_st.append(ke if i == nc - 1 else ke * row_factor(b_end[nc - 1] - b_end[i]))
        attn_rows = []
        for i in range(nc):
            pieces = []
            for j in range(i):
                pieces.append(k_end[j] if j == i - 1
                              else k_end[j] * row_factor(b_end[i - 1] - b_end[j]))
            pieces.append(k_inv[i])
            if i < nc - 1:
                pieces.append(jnp.zeros(((nc - 1 - i) * GLA_CHUNK, GLA_HEAD_K), BF16))
            attn_rows.append(lax.dot_general(q_dec[i], jnp.concatenate(pieces, axis=0), nt_dims,
                                             preferred_element_type=F32))
        attn = jnp.where(causal, jnp.concatenate(attn_rows, axis=0), 0.0).astype(BF16)
        st = st_scr[bb, h]
        v = v_ref[bb, :, vs]
        o = jnp.dot(jnp.concatenate([attn] + [jnp.concatenate(q_int, axis=0)], axis=1),
                    jnp.concatenate([v, st.astype(BF16)], axis=0),
                    preferred_element_type=F32)
        dec_col = jnp.broadcast_to(jnp.exp(b_end[nc - 1]), (LANES, GLA_HEAD_K)).T
        dec = jnp.concatenate([dec_col] * (GLA_HEAD_V // LANES), axis=1)
        st_scr[bb, h] = st * dec + lax.dot_general(
            jnp.concatenate(k_st, axis=0), v, (((0,), (0,)), ((), ())),
            preferred_element_type=F32)
        ms = jnp.mean(o * o, axis=-1, keepdims=True)
        z = z_ref[bb, :, vs]
        gate = (z * jax.nn.sigmoid(z)).astype(F32)
        o_ref[bb, :, vs] = (o * lax.rsqrt(ms + EPS) * nw * gate).astype(BF16)


def _gla(main, glow, wg_bf, bg, tri, nw):
    nt = SEQ // GLA_ROWS
    bps = GLA_BATCH_PER_STEP
    assert 2 * GLA_DK == D_INNER and BATCH % bps == 0
    main3 = main.reshape(BATCH, SEQ, main.shape[1])
    glow3 = glow.reshape(BATCH, SEQ, LANES)
    out = pl.pallas_call(
        _gla_kernel,
        grid=(BATCH // bps, nt),
        in_specs=[
            pl.BlockSpec((bps, GLA_ROWS, GLA_DK), lambda b, t: (b, t, 0)),
            pl.BlockSpec((bps, GLA_ROWS, GLA_DK), lambda b, t: (b, t, 1)),
            pl.BlockSpec((bps, GLA_ROWS, D_INNER), lambda b, t: (b, t, 1)),
            pl.BlockSpec((bps, GLA_ROWS, D_INNER), lambda b, t: (b, t, 2)),
            pl.BlockSpec((bps, GLA_ROWS, LANES), lambda b, t: (b, t, 0)),
            pl.BlockSpec((LANES, GLA_DK), lambda b, t: (0, 0)),
            pl.BlockSpec((1, GLA_DK), lambda b, t: (0, 0)),
            pl.BlockSpec((GLA_ROWS, GLA_ROWS), lambda b, t: (0, 0)),
            pl.BlockSpec((1, GLA_HEAD_V), lambda b, t: (0, 0)),
        ],
        out_specs=pl.BlockSpec((bps, GLA_ROWS, D_INNER), lambda b, t: (b, t, 0)),
        out_shape=jax.ShapeDtypeStruct((BATCH, SEQ, D_INNER), BF16),
        scratch_shapes=[pltpu.VMEM((bps, GLA_HEADS, GLA_HEAD_K, GLA_HEAD_V), F32)],
        compiler_params=_cparams(("parallel", "arbitrary")),
    )(main3, main3, main3, main3, glow3, wg_bf, bg, tri, nw)
    return out.reshape(M_ROWS, D_INNER)


def _prep_da_w_kernel(w_ref, o_ref, *, n_qk_tiles):
    j = pl.program_id(0)

    @pl.when(j < n_qk_tiles)
    def _():
        lane = lax.broadcasted_iota(jnp.int32, (w_ref.shape[0], LANES), 1)
        take_next = (lane >= ROPE_HALF) & (lane < 2 * ROPE_HALF)
        take_prev = (lane >= 2 * ROPE_HALF) & (lane < 3 * ROPE_HALF)
        for gidx in range(w_ref.shape[1] // LANES):
            cols = slice(gidx * LANES, (gidx + 1) * LANES)
            w = w_ref[:, cols]
            nxt = pltpu.roll(w, LANES - ROPE_HALF, 1)
            prv = pltpu.roll(w, ROPE_HALF, 1)
            o_ref[:, cols] = jnp.where(take_next, nxt,
                                       jnp.where(take_prev, prv, w)).astype(BF16)

    @pl.when(j >= n_qk_tiles)
    def _():
        o_ref[...] = w_ref[...].astype(BF16)


def _prep_da_w(w_in):
    d, n = w_in.shape
    kern = functools.partial(_prep_da_w_kernel, n_qk_tiles=2 * D_INNER // TN_IN)
    return pl.pallas_call(
        kern,
        grid=(n // TN_IN,),
        in_specs=[pl.BlockSpec((d, TN_IN), lambda j: (0, j))],
        out_specs=pl.BlockSpec((d, TN_IN), lambda j: (0, j)),
        out_shape=jax.ShapeDtypeStruct((d, n), BF16),
        compiler_params=_cparams(("parallel",)),
    )(w_in)


def kernel(x, c, positions, ada_w, ada_b, norm_g, da_w_in, da_lam_q1, da_lam_k1, da_lam_q2,
           da_lam_k2, da_subln_w, da_w_out, gla_w_in, gla_w_gate_up, gla_b_gate, gla_norm_w,
           gla_w_out, final_g):
    x2 = x.reshape(M_ROWS, D_MODEL)
    c_pad = jnp.pad(c, ((0, ADA_ROWS - BATCH), (0, 0))).astype(BF16)
    mod = _ada(c_pad, ada_w, ada_b.reshape(DEPTH, 1, 3 * D_MODEL))
    mod = mod.reshape(DEPTH * ADA_ROWS * 3, 1, D_MODEL)
    norm_g3 = norm_g.reshape(DEPTH, 1, D_MODEL)

    inv_freq = ROPE_THETA ** (-jnp.arange(0, DA_HEAD_DIM, 2, dtype=F32) / DA_HEAD_DIM)
    cos, sin = _rope_tables(positions.reshape(BATCH, 1, SEQ), inv_freq.reshape(ROPE_HALF, 1))

    lambda_init = 0.8 - 0.6 * math.exp(-0.3 * 0)
    qkvz = _inproj_da(x2, norm_g3, mod, _prep_da_w(da_w_in[0]), cos, sin, layer=0)
    lamv = jnp.concatenate([da_lam_q1, da_lam_k1, da_lam_q2, da_lam_k2], axis=0)
    lamv = jnp.pad(lamv, ((0, SUBLANES - lamv.shape[0]), (0, LANES - DA_HEAD_DIM)))
    o = _diff_attn(qkvz, lamv, da_subln_w[0].reshape(DA_V_DIM, 1), lambda_init)
    x2 = _outproj(o, da_w_out[0].astype(BF16), x2, mod, final_g.reshape(1, D_MODEL),
                  layer=0, final_norm=False)

    w = gla_w_in[0]
    wg_in = jnp.pad(w[:, GLA_MAIN:], ((0, 0), (0, LANES - GLA_GATE_RANK))).astype(BF16)
    main, glow = _inproj_gla(x2, norm_g3, mod, w.astype(BF16), wg_in, layer=1)
    wg_up = jnp.pad(gla_w_gate_up[0], ((0, LANES - GLA_GATE_RANK), (0, 0))).astype(BF16)
    r = jnp.arange(GLA_ROWS)
    tri = (r[:, None] >= r[None, :]).astype(BF16)
    o = _gla(main, glow, wg_up, gla_b_gate, tri, gla_norm_w)
    out = _outproj(o, gla_w_out[0].astype(BF16), x2, mod, final_g.reshape(1, D_MODEL),
                   layer=1, final_norm=True)
    return out.reshape(BATCH, SEQ, D_MODEL)
```
